```python
import jax, jax.numpy as jnp
from jax import lax
import numpy as np

D_MODEL = 2048
BATCH = 16
SEQ = 2048
DEPTH = 4

D_MIX = D_MODEL
D_CONV = D_MIX // 2
D_HGRN = D_MIX - D_CONV
CONV_GROUPS = 8
CONV_GROUP_DIM = D_CONV // CONV_GROUPS
CONV_WIDTH = 3
HGRN_HEADS = 8
HGRN_KEY_DIM = D_HGRN // HGRN_HEADS
HGRN_VAL_DIM = D_HGRN // HGRN_HEADS
CHUNK = 64
D_IN = 3 * D_CONV + 4 * D_HGRN
D_FF = -(-8 * D_MODEL // (3 * 256)) * 256
EPS = 1e-6
MIN_FORGET = 1e-30

kernel_name = "hybrid_conv_hgrn2_parallel_trunk"


def rms_norm(x, gain):
    xf = x.astype(jnp.float32)
    y = xf * lax.rsqrt(jnp.mean(xf * xf, axis=-1, keepdims=True) + EPS)
    return (y * gain.astype(jnp.float32)).astype(x.dtype)


def short_conv_mixer(gate_b, gate_c, h, conv_w, gn):
    bsz, length, _ = h.shape
    u = gate_c * h
    up = jnp.pad(u, ((0, 0), (CONV_WIDTH - 1, 0), (0, 0)))
    y = up[:, 0:length] * conv_w[0]
    for j in range(1, CONV_WIDTH):
        y = y + up[:, j:j + length] * conv_w[j]
    y = gate_b * y
    y = rms_norm(y.reshape(bsz, length, CONV_GROUPS, CONV_GROUP_DIM),
                 gn.reshape(CONV_GROUPS, CONV_GROUP_DIM))
    return y.reshape(bsz, length, D_CONV)


def hgrn2_mixer(q, z, v, g, lb, gn):
    bsz, length, _ = q.shape
    n_chunks = length // CHUNK
    out_dtype = v.dtype
    qf = jax.nn.silu(q.astype(jnp.float32))
    zf = z.astype(jnp.float32)
    lbf = lb.astype(jnp.float32)
    sig = jax.nn.sigmoid(zf)
    f = lbf + (1.0 - lbf) * sig
    log_f = jnp.log(jnp.maximum(f, MIN_FORGET))
    k = (1.0 - lbf) * jax.nn.sigmoid(-zf)
    vf = v.astype(jnp.float32)

    def to_chunks(t, d):
        return t.reshape(bsz, n_chunks, CHUNK, HGRN_HEADS, d).transpose(1, 0, 3, 2, 4)

    qc = to_chunks(qf, HGRN_KEY_DIM)
    kc = to_chunks(k, HGRN_KEY_DIM)
    lfc = to_chunks(log_f, HGRN_KEY_DIM)
    vc = to_chunks(vf, HGRN_VAL_DIM)
    causal = jnp.tril(jnp.ones((CHUNK, CHUNK), dtype=bool))[:, :, None]

    def step(state, inp):
        qi, ki, vi, lfi = inp
        b = jnp.cumsum(lfi, axis=-2)
        inter = jnp.einsum('bhtk,bhkv->bhtv', qi * jnp.exp(b), state)
        diff = b[:, :, :, None, :] - b[:, :, None, :, :]
        decay = jnp.where(causal, jnp.exp(jnp.minimum(diff, 0.0)), 0.0)
        scores = jnp.einsum('bhtk,bhsk,bhtsk->bhts', qi, ki, decay)
        o = inter + jnp.einsum('bhts,bhsv->bhtv', scores, vi)
        b_last = b[:, :, -1:, :]
        new_state = (jnp.exp(b_last[:, :, 0, :])[..., None] * state
                     + jnp.einsum('bhsk,bhsv->bhkv', ki * jnp.exp(b_last - b), vi))
        return new_state, o

    s0 = jnp.zeros((bsz, HGRN_HEADS, HGRN_KEY_DIM, HGRN_VAL_DIM), jnp.float32)
    _, oc = lax.scan(step, s0, (qc, kc, vc, lfc))
    o = oc.transpose(1, 0, 3, 2, 4).reshape(bsz, length, HGRN_HEADS, HGRN_VAL_DIM)
    o = rms_norm(o, gn.reshape(HGRN_HEADS, HGRN_VAL_DIM)).reshape(bsz, length, D_HGRN)
    return (o * jax.nn.silu(g.astype(jnp.float32))).astype(out_dtype)


def setup_inputs(seed: int = 0) -> dict:
    key = jax.random.key(seed)
    ks = jax.random.split(key, 16)
    f32 = jnp.float32

    def nrm(k, shape, scale):
        return jax.random.normal(k, shape, f32) * scale

    return {
        "x": jax.random.normal(ks[0], (BATCH, SEQ, D_MODEL), f32),
        "norm_mix": 1.0 + nrm(ks[1], (DEPTH, D_MODEL), 0.02),
        "w_in": nrm(ks[2], (DEPTH, D_MODEL, D_IN), D_MODEL ** -0.5),
        "conv_w": nrm(ks[3], (DEPTH, CONV_WIDTH, D_CONV), CONV_WIDTH ** -0.5),
        "gn_conv": 1.0 + nrm(ks[4], (DEPTH, D_CONV), 0.02),
        "lb_logits": nrm(ks[5], (DEPTH, D_HGRN), 0.1),
        "gn_hgrn": 1.0 + nrm(ks[6], (DEPTH, D_HGRN), 0.02),
        "w_out": nrm(ks[7], (DEPTH, D_MIX, D_MODEL), D_MIX ** -0.5),
        "norm_ffn": 1.0 + nrm(ks[8], (DEPTH, D_MODEL), 0.02),
        "w_gate": nrm(ks[9], (DEPTH, D_MODEL, D_FF), D_MODEL ** -0.5),
        "w_up": nrm(ks[10], (DEPTH, D_MODEL, D_FF), D_MODEL ** -0.5),
        "w_down": nrm(ks[11], (DEPTH, D_FF, D_MODEL), D_FF ** -0.5),
        "norm_final": 1.0 + nrm(ks[12], (D_MODEL,), 0.02),
    }


def reference(x, norm_mix, w_in, conv_w, gn_conv, lb_logits, gn_hgrn, w_out,
              norm_ffn, w_gate, w_up, w_down, norm_final):
    p = jax.nn.softmax(lb_logits.astype(jnp.float32), axis=0)
    lower_bounds = jnp.clip(jnp.cumsum(p, axis=0) - p[0], 0.0, 1.0 - 1e-4)

    c0, c1, c2 = D_CONV, 2 * D_CONV, 3 * D_CONV
    h0, h1, h2 = c2 + D_HGRN, c2 + 2 * D_HGRN, c2 + 3 * D_HGRN
    for l in range(DEPTH):
        h = rms_norm(x, norm_mix[l])
        proj = h @ w_in[l]
        conv_out = short_conv_mixer(proj[..., :c0], proj[..., c0:c1], proj[..., c1:c2],
                                    conv_w[l], gn_conv[l])
        hgrn_out = hgrn2_mixer(proj[..., c2:h0], proj[..., h0:h1], proj[..., h1:h2],
                               proj[..., h2:], lower_bounds[l], gn_hgrn[l])
        mixed = jnp.concatenate([conv_out, hgrn_out], axis=-1)
        x = x + mixed @ w_out[l]
        h = rms_norm(x, norm_ffn[l])
        x = x + (jax.nn.silu(h @ w_gate[l]) * (h @ w_up[l])) @ w_down[l]
    return rms_norm(x, norm_final)
```

```python
import functools

import jax
import jax.numpy as jnp
from jax import lax
from jax.experimental import pallas as pl
from jax.experimental.pallas import tpu as pltpu

EPS = 1e-6
MIN_FORGET = 1e-30
CONV_GROUPS = 8
HGRN_HEADS = 8
LANES = 128
SUBLANES = 8
VMEM_LIMIT = 56 * 1024 * 1024

F32 = jnp.float32
BF16 = jnp.bfloat16


def _params(*sem):
    return pltpu.CompilerParams(dimension_semantics=sem, vmem_limit_bytes=VMEM_LIMIT)


def _pick(n, pref):
    t = min(pref, n)
    while n % t:
        t -= LANES
    return t


def _rms(xf, gain):
    ms = jnp.mean(xf * xf, axis=-1, keepdims=True)
    return xf * lax.rsqrt(ms + EPS) * gain


def _norm_kernel(x_ref, g_ref, o_ref):
    o_ref[...] = _rms(x_ref[...], g_ref[...]).astype(o_ref.dtype)


def _norm(x, gain, out_dtype, tm=512):
    t, d = x.shape
    tm = _pick(t, tm)
    return pl.pallas_call(
        _norm_kernel,
        grid=(t // tm,),
        in_specs=[pl.BlockSpec((tm, d), lambda i: (i, 0)),
                  pl.BlockSpec((1, d), lambda i: (0, 0))],
        out_specs=pl.BlockSpec((tm, d), lambda i: (i, 0)),
        out_shape=jax.ShapeDtypeStruct((t, d), out_dtype),
        compiler_params=_params("parallel"),
        name="rms_norm",
    )(x, gain.reshape(1, d))


def _mm_kernel(a_ref, w_ref, o_ref):
    o_ref[...] = jnp.dot(a_ref[...], w_ref[...],
                         preferred_element_type=F32).astype(o_ref.dtype)


def _matmul(a, w, out_dtype, tm=1024, tn=1024):
    t, k = a.shape
    n = w.shape[1]
    tm, tn = _pick(t, tm), _pick(n, tn)
    return pl.pallas_call(
        _mm_kernel,
        grid=(t // tm, n // tn),
        in_specs=[pl.BlockSpec((tm, k), lambda i, j: (i, 0)),
                  pl.BlockSpec((k, tn), lambda i, j: (0, j))],
        out_specs=pl.BlockSpec((tm, tn), lambda i, j: (i, j)),
        out_shape=jax.ShapeDtypeStruct((t, n), out_dtype),
        compiler_params=_params("parallel", "arbitrary"),
        name="in_proj",
    )(a, w)


def _glu_kernel(a_ref, wg_ref, wu_ref, o_ref):
    a = a_ref[...]
    g = jnp.dot(a, wg_ref[...], preferred_element_type=F32)
    u = jnp.dot(a, wu_ref[...], preferred_element_type=F32)
    o_ref[...] = (g * jax.nn.sigmoid(g) * u).astype(o_ref.dtype)


def _glu(a, wg, wu, tm=1024, tn=512):
    t, k = a.shape
    n = wg.shape[1]
    tm, tn = _pick(t, tm), _pick(n, tn)
    return pl.pallas_call(
        _glu_kernel,
        grid=(t // tm, n // tn),
        in_specs=[pl.BlockSpec((tm, k), lambda i, j: (i, 0)),
                  pl.BlockSpec((k, tn), lambda i, j: (0, j)),
                  pl.BlockSpec((k, tn), lambda i, j: (0, j))],
        out_specs=pl.BlockSpec((tm, tn), lambda i, j: (i, j)),
        out_shape=jax.ShapeDtypeStruct((t, n), BF16),
        compiler_params=_params("parallel", "arbitrary"),
        name="ffn_glu",
    )(a, wg, wu)


def _out_kernel(a1_ref, a2_ref, w1_ref, w2_ref, r_ref, g_ref, x_ref, h_ref):
    acc = jnp.dot(a1_ref[...], w1_ref[...], preferred_element_type=F32)
    acc = acc + jnp.dot(a2_ref[...], w2_ref[...], preferred_element_type=F32)
    xn = r_ref[...] + acc
    x_ref[...] = xn
    h_ref[...] = _rms(xn, g_ref[...]).astype(h_ref.dtype)


def _out_proj(a1, a2, w, res, gain, tm=512):
    t, k1 = a1.shape
    k2 = a2.shape[1]
    d = w.shape[1]
    assert k1 == k2
    tm = _pick(t, tm)
    return pl.pallas_call(
        _out_kernel,
        grid=(t // tm,),
        in_specs=[pl.BlockSpec((tm, k1), lambda i: (i, 0)),
                  pl.BlockSpec((tm, k2), lambda i: (i, 0)),
                  pl.BlockSpec((k1, d), lambda i: (0, 0)),
                  pl.BlockSpec((k2, d), lambda i: (1, 0)),
                  pl.BlockSpec((tm, d), lambda i: (i, 0)),
                  pl.BlockSpec((1, d), lambda i: (0, 0))],
        out_specs=[pl.BlockSpec((tm, d), lambda i: (i, 0)),
                   pl.BlockSpec((tm, d), lambda i: (i, 0))],
        out_shape=[jax.ShapeDtypeStruct((t, d), F32),
                   jax.ShapeDtypeStruct((t, d), BF16)],
        input_output_aliases={4: 0},
        compiler_params=_params("parallel"),
        name="out_proj",
    )(a1, a2, w, w, res, gain.reshape(1, d))


def _down_kernel(emit_x, a_ref, w_ref, r_ref, g_ref, *rest):
    if emit_x:
        x_ref, h_ref, acc_ref = rest
    else:
        h_ref, acc_ref = rest
    kk = pl.program_id(1)
    part = jnp.dot(a_ref[...], w_ref[...], preferred_element_type=F32)

    @pl.when(kk == 0)
    def _():
        acc_ref[...] = part

    @pl.when(kk != 0)
    def _():
        acc_ref[...] += part

    @pl.when(kk == pl.num_programs(1) - 1)
    def _():
        xn = r_ref[...] + acc_ref[...]
        if emit_x:
            x_ref[...] = xn
        h_ref[...] = _rms(xn, g_ref[...]).astype(h_ref.dtype)


def _down_proj(a, w, res, gain, last, tm=512, tk=512):
    t, k = a.shape
    d = w.shape[1]
    tm, tk = _pick(t, tm), _pick(k, tk)
    row = pl.BlockSpec((tm, d), lambda i, kk: (i, 0))
    if last:
        out_specs = row
        out_shape = jax.ShapeDtypeStruct((t, d), F32)
        aliases = {}
    else:
        out_specs = [row, row]
        out_shape = [jax.ShapeDtypeStruct((t, d), F32), jax.ShapeDtypeStruct((t, d), BF16)]
        aliases = {2: 0}
    return pl.pallas_call(
        functools.partial(_down_kernel, not last),
        grid=(t // tm, k // tk),
        in_specs=[pl.BlockSpec((tm, tk), lambda i, kk: (i, kk)),
                  pl.BlockSpec((tk, d), lambda i, kk: (kk, 0)),
                  row,
                  pl.BlockSpec((1, d), lambda i, kk: (0, 0))],
        out_specs=out_specs,
        out_shape=out_shape,
        scratch_shapes=[pltpu.VMEM((tm, d), F32)],
        input_output_aliases=aliases,
        compiler_params=_params("parallel", "arbitrary"),
        name="down_proj",
    )(a, w, res, gain.reshape(1, d))


def _conv_kernel(gb_ref, gc_ref, hv_ref, cw_ref, gn_ref, o_ref, tail_ref):
    ct = o_ref.shape[0]

    @pl.when(pl.program_id(1) == 0)
    def _():
        tail_ref[...] = jnp.zeros_like(tail_ref)

    u = gc_ref[...].astype(F32) * hv_ref[...].astype(F32)
    full = jnp.concatenate([tail_ref[...], u], axis=0)
    u1 = pltpu.roll(full, 1, axis=0)[SUBLANES:]
    u2 = pltpu.roll(full, 2, axis=0)[SUBLANES:]
    tail_ref[...] = u[ct - SUBLANES:]
    w = cw_ref[...]
    y = u2 * w[0:1] + u1 * w[1:2] + u * w[2:3]
    y = gb_ref[...].astype(F32) * y
    gd = y.shape[1] // CONV_GROUPS
    for gi in range(CONV_GROUPS):
        sl = slice(gi * gd, (gi + 1) * gd)
        o_ref[:, sl] = _rms(y[:, sl], gn_ref[:, sl]).astype(o_ref.dtype)


def _conv_mixer(proj, conv_w, gn, bsz, length, ct=512):
    t = proj.shape[0]
    dc = conv_w.shape[1]
    ct = _pick(length, ct)
    nc = length // ct

    def col(j):
        return pl.BlockSpec((ct, dc), lambda b, c: (b * nc + c, j))

    return pl.pallas_call(
        _conv_kernel,
        grid=(bsz, nc),
        in_specs=[col(0), col(1), col(2),
                  pl.BlockSpec(conv_w.shape, lambda b, c: (0, 0)),
                  pl.BlockSpec((1, dc), lambda b, c: (0, 0))],
        out_specs=pl.BlockSpec((ct, dc), lambda b, c: (b * nc + c, 0)),
        out_shape=jax.ShapeDtypeStruct((t, dc), BF16),
        scratch_shapes=[pltpu.VMEM((SUBLANES, dc), F32)],
        compiler_params=_params("parallel", "arbitrary"),
        name="conv_mixer",
    )(proj, proj, proj, conv_w, gn.reshape(1, dc))


def _row_gather(b, m):
    c, d = b.shape
    if m == 1:
        row = lax.broadcasted_iota(jnp.int32, (c, d), 0)
        return jnp.where((row & 1) == 1, pltpu.roll(b, 1, axis=0), b)
    if m == 2:
        b3 = b.reshape(c // 8, 8, d)
        sub = lax.broadcasted_iota(jnp.int32, b3.shape, 1)
        lo = jnp.broadcast_to(b3[:, 1:2, :], b3.shape)
        hi = jnp.broadcast_to(b3[:, 5:6, :], b3.shape)
        return jnp.where(sub < 4, lo, hi).reshape(c, d)
    bm = b.reshape(c // (2 * m), 2 * m, d)
    return jnp.broadcast_to(bm[:, m - 1:m, :], bm.shape).reshape(c, d)


def _hgrn_chunk(q, z, v, g, lb, gn, st):
    c, dk = q.shape
    row = lax.broadcasted_iota(jnp.int32, (c, dk), 0)

    qf = q * jax.nn.sigmoid(q)
    e = jnp.exp(-jnp.abs(z))
    r = 1.0 / (1.0 + e)
    er = e * r
    pos = z >= 0
    sig = jnp.where(pos, r, er)
    nsig = jnp.where(pos, er, r)
    one_m_lb = 1.0 - lb
    f = lb + one_m_lb * sig
    lf = jnp.log(jnp.maximum(f, MIN_FORGET))
    k = one_m_lb * nsig

    b = lf
    s = 1
    while s < c:
        b = b + jnp.where(row >= s, pltpu.roll(b, s, axis=0), 0.0)
        s *= 2
    b_last = b[c - 1:c, :]

    trow = lax.broadcasted_iota(jnp.int32, (c, c), 0)
    scol = lax.broadcasted_iota(jnp.int32, (c, c), 1)
    tx = trow ^ scol
    scores = jnp.zeros((c, c), F32)
    m = 1
    while m < c:
        em = jnp.exp(-jnp.abs(b - _row_gather(b, m)))
        is_q = (row & m) != 0
        qm = jnp.where(is_q, qf * em, 0.0).astype(BF16)
        km = jnp.where(is_q, 0.0, k * em).astype(BF16)
        sm = lax.dot_general(qm, km, (((1,), (1,)), ((), ())), preferred_element_type=F32)
        scores = scores + jnp.where(tx < 2 * m, sm, 0.0)
        m *= 2

    diag = jnp.sum(qf * k, axis=-1, keepdims=True)
    v_bf = v.astype(BF16)
    qe = (qf * jnp.exp(b)).astype(BF16)
    inter = lax.dot_general(qe, st.astype(BF16), (((1,), (1,)), ((), ())),
                            preferred_element_type=F32)
    o = inter + jnp.dot(scores.astype(BF16), v_bf, preferred_element_type=F32) + diag * v

    khat = (k * jnp.exp(b_last - b)).astype(BF16)
    st_new = st * jnp.exp(b_last) + lax.dot_general(
        v_bf, khat, (((0,), (0,)), ((), ())), preferred_element_type=F32)

    out = _rms(o, gn) * (g * jax.nn.sigmoid(g))
    return out, st_new


def _hgrn_kernel(layer, c, q_ref, z_ref, v_ref, g_ref, lbl_ref, gn_ref, o_ref, st_ref):
    ct = o_ref.shape[0]

    @pl.when(pl.program_id(2) == 0)
    def _():
        st_ref[...] = jnp.zeros_like(st_ref)

    lg = lbl_ref[...]
    ex = jnp.exp(lg - jnp.max(lg, axis=0, keepdims=True))
    p = ex / jnp.sum(ex, axis=0, keepdims=True)
    cum = jnp.sum(p[:layer + 1], axis=0, keepdims=True)
    lb = jnp.clip(cum - p[0:1], 0.0, 1.0 - 1e-4)
    gn = gn_ref[...]

    def body(i, carry):
        r0 = pl.multiple_of(i * c, c)
        rows = pl.ds(r0, c)
        out, st_new = _hgrn_chunk(q_ref[rows, :].astype(F32), z_ref[rows, :].astype(F32),
                                  v_ref[rows, :].astype(F32), g_ref[rows, :].astype(F32),
                                  lb, gn, st_ref[...])
        st_ref[...] = st_new
        o_ref[rows, :] = out.astype(o_ref.dtype)
        return carry

    lax.fori_loop(0, ct // c, body, 0)


def _hgrn_mixer(proj, lb_logits, gn, layer, bsz, length, col0, ct=512, c=128):
    t = proj.shape[0]
    dh = lb_logits.shape[1]
    dk = dh // HGRN_HEADS
    ct = _pick(length, ct)
    c = min(c, ct)
    nc = length // ct
    depth = lb_logits.shape[0]

    def col(j):
        return pl.BlockSpec((ct, dk), lambda b, h, cc: (b * nc + cc, col0 + j * HGRN_HEADS + h))

    return pl.pallas_call(
        functools.partial(_hgrn_kernel, layer, c),
        grid=(bsz, HGRN_HEADS, nc),
        in_specs=[col(0), col(1), col(2), col(3),
                  pl.BlockSpec((depth, dk), lambda b, h, cc: (0, h)),
                  pl.BlockSpec((1, dk), lambda b, h, cc: (0, h))],
        out_specs=pl.BlockSpec((ct, dk), lambda b, h, cc: (b * nc + cc, h)),
        out_shape=jax.ShapeDtypeStruct((t, dh), BF16),
        scratch_shapes=[pltpu.VMEM((dk, dk), F32)],
        compiler_params=_params("parallel", "parallel", "arbitrary"),
        name="hgrn_mixer",
    )(proj, proj, proj, proj, lb_logits, gn.reshape(1, dh))


def kernel(x, norm_mix, w_in, conv_w, gn_conv, lb_logits, gn_hgrn, w_out, norm_ffn,
           w_gate, w_up, w_down, norm_final):
    bsz, length, d = x.shape
    depth = w_in.shape[0]
    dc = conv_w.shape[-1]
    dh = lb_logits.shape[-1]
    t = bsz * length
    hgrn_col0 = (3 * dc) // (dh // HGRN_HEADS)

    xr = x.reshape(t, d)
    h = _norm(xr, norm_mix[0], BF16)
    for l in range(depth):
        proj = _matmul(h, w_in[l].astype(BF16), BF16)
        conv_out = _conv_mixer(proj, conv_w[l], gn_conv[l], bsz, length)
        hgrn_out = _hgrn_mixer(proj, lb_logits, gn_hgrn[l], l, bsz, length, hgrn_col0)
        xr, h = _out_proj(conv_out, hgrn_out, w_out[l].astype(BF16), xr, norm_ffn[l])
        act = _glu(h, w_gate[l].astype(BF16), w_up[l].astype(BF16))
        last = l == depth - 1
        if last:
            y = _down_proj(act, w_down[l].astype(BF16), xr, norm_final, True)
        else:
            xr, h = _down_proj(act, w_down[l].astype(BF16), xr, norm_mix[l + 1], False)
    return y.reshape(bsz, length, d)
```

```python
import functools

import jax
import jax.numpy as jnp
from jax import lax
from jax.experimental import pallas as pl
from jax.experimental.pallas import tpu as pltpu

EPS = 1e-6
MIN_FORGET = 1e-30
LOG2E = 1.4426950408889634
CONV_GROUPS = 8
HGRN_HEADS = 8
LANES = 128
SUBLANES = 8
VMEM_LIMIT = 56 * 1024 * 1024
HGRN_CHUNK = 128

F32 = jnp.float32
BF16 = jnp.bfloat16


def _params(*sem):
    return pltpu.CompilerParams(dimension_semantics=sem, vmem_limit_bytes=VMEM_LIMIT)


def _pick(n, pref):
    t = min(pref, n)
    while n % t:
        t -= LANES
    return t


def _rms(xf, gain):
    ms = jnp.mean(xf * xf, axis=-1, keepdims=True)
    return xf * lax.rsqrt(ms + EPS) * gain


def _silu(x):
    return x * (1.0 / (1.0 + jnp.exp2(x * (-LOG2E))))


def _dot_t(a, b):
    return lax.dot_general(a, b, (((1,), (1,)), ((), ())), preferred_element_type=F32)


def _norm_kernel(x_ref, g_ref, o_ref):
    o_ref[...] = _rms(x_ref[...], g_ref[...]).astype(o_ref.dtype)


def _norm(x, gain, out_dtype, tm=512):
    t, d = x.shape
    tm = _pick(t, tm)
    return pl.pallas_call(
        _norm_kernel,
        grid=(t // tm,),
        in_specs=[pl.BlockSpec((tm, d), lambda i: (i, 0)),
                  pl.BlockSpec((1, d), lambda i: (0, 0))],
        out_specs=pl.BlockSpec((tm, d), lambda i: (i, 0)),
        out_shape=jax.ShapeDtypeStruct((t, d), out_dtype),
        compiler_params=_params("parallel"),
        name="rms_norm",
    )(x, gain.reshape(1, d))


def _mm_kernel(a_ref, w_ref, o_ref, wb_ref):
    @pl.when(pl.program_id(1) == 0)
    def _():
        wb_ref[...] = w_ref[...].astype(BF16)

    o_ref[...] = jnp.dot(a_ref[...], wb_ref[...],
                         preferred_element_type=F32).astype(o_ref.dtype)


def _matmul(a, w, out_dtype, tm=1024, tn=1024):
    t, k = a.shape
    n = w.shape[1]
    tm, tn = _pick(t, tm), _pick(n, tn)
    return pl.pallas_call(
        _mm_kernel,
        grid=(n // tn, t // tm),
        in_specs=[pl.BlockSpec((tm, k), lambda j, i: (i, 0)),
                  pl.BlockSpec((k, tn), lambda j, i: (0, j))],
        out_specs=pl.BlockSpec((tm, tn), lambda j, i: (i, j)),
        out_shape=jax.ShapeDtypeStruct((t, n), out_dtype),
        scratch_shapes=[pltpu.VMEM((k, tn), BF16)],
        compiler_params=_params("arbitrary", "arbitrary"),
        name="in_proj",
    )(a, w)


def _glu_kernel(a_ref, wg_ref, wu_ref, o_ref, wgb_ref, wub_ref):
    @pl.when(pl.program_id(1) == 0)
    def _():
        wgb_ref[...] = wg_ref[...].astype(BF16)
        wub_ref[...] = wu_ref[...].astype(BF16)

    a = a_ref[...]
    g = jnp.dot(a, wgb_ref[...], preferred_element_type=F32)
    u = jnp.dot(a, wub_ref[...], preferred_element_type=F32)
    o_ref[...] = (_silu(g) * u).astype(o_ref.dtype)


def _glu(a, wg, wu, tm=1024, tn=512):
    t, k = a.shape
    n = wg.shape[1]
    tm, tn = _pick(t, tm), _pick(n, tn)
    return pl.pallas_call(
        _glu_kernel,
        grid=(n // tn, t // tm),
        in_specs=[pl.BlockSpec((tm, k), lambda j, i: (i, 0)),
                  pl.BlockSpec((k, tn), lambda j, i: (0, j)),
                  pl.BlockSpec((k, tn), lambda j, i: (0, j))],
        out_specs=pl.BlockSpec((tm, tn), lambda j, i: (i, j)),
        out_shape=jax.ShapeDtypeStruct((t, n), BF16),
        scratch_shapes=[pltpu.VMEM((k, tn), BF16), pltpu.VMEM((k, tn), BF16)],
        compiler_params=_params("arbitrary", "arbitrary"),
        name="ffn_glu",
    )(a, wg, wu)


def _out_kernel(a1_ref, a2_ref, w1_ref, w2_ref, r_ref, g_ref, x_ref, h_ref):
    acc = jnp.dot(a1_ref[...], w1_ref[...], preferred_element_type=F32)
    acc = acc + jnp.dot(a2_ref[...], w2_ref[...], preferred_element_type=F32)
    xn = r_ref[...] + acc
    x_ref[...] = xn
    h_ref[...] = _rms(xn, g_ref[...]).astype(h_ref.dtype)


def _out_proj(a1, a2, w, res, gain, tm=512):
    t, k1 = a1.shape
    k2 = a2.shape[1]
    d = w.shape[1]
    assert k1 == k2
    tm = _pick(t, tm)
    return pl.pallas_call(
        _out_kernel,
        grid=(t // tm,),
        in_specs=[pl.BlockSpec((tm, k1), lambda i: (i, 0)),
                  pl.BlockSpec((tm, k2), lambda i: (i, 0)),
                  pl.BlockSpec((k1, d), lambda i: (0, 0)),
                  pl.BlockSpec((k2, d), lambda i: (1, 0)),
                  pl.BlockSpec((tm, d), lambda i: (i, 0)),
                  pl.BlockSpec((1, d), lambda i: (0, 0))],
        out_specs=[pl.BlockSpec((tm, d), lambda i: (i, 0)),
                   pl.BlockSpec((tm, d), lambda i: (i, 0))],
        out_shape=[jax.ShapeDtypeStruct((t, d), F32),
                   jax.ShapeDtypeStruct((t, d), BF16)],
        compiler_params=_params("parallel"),
        name="out_proj",
    )(a1, a2, w, w, res, gain.reshape(1, d))


def _down_kernel(emit_x, a_ref, w_ref, r_ref, g_ref, *outs):
    xn = r_ref[...] + jnp.dot(a_ref[...], w_ref[...], preferred_element_type=F32)
    if emit_x:
        outs[0][...] = xn
    outs[-1][...] = _rms(xn, g_ref[...]).astype(outs[-1].dtype)


def _down_proj(a, w, res, gain, last, tm=256):
    t, k = a.shape
    d = w.shape[1]
    tm = _pick(t, tm)
    row = pl.BlockSpec((tm, d), lambda i: (i, 0))
    if last:
        out_specs = row
        out_shape = jax.ShapeDtypeStruct((t, d), F32)
    else:
        out_specs = [row, row]
        out_shape = [jax.ShapeDtypeStruct((t, d), F32), jax.ShapeDtypeStruct((t, d), BF16)]
    return pl.pallas_call(
        functools.partial(_down_kernel, not last),
        grid=(t // tm,),
        in_specs=[pl.BlockSpec((tm, k), lambda i: (i, 0)),
                  pl.BlockSpec((k, d), lambda i: (0, 0), pipeline_mode=pl.Buffered(1)),
                  row,
                  pl.BlockSpec((1, d), lambda i: (0, 0))],
        out_specs=out_specs,
        out_shape=out_shape,
        compiler_params=_params("parallel"),
        name="down_proj",
    )(a, w, res, gain.reshape(1, d))


def _conv_kernel(gb_ref, gc_ref, hv_ref, cw_ref, gn_ref, o_ref, tail_ref):
    ct = o_ref.shape[0]

    @pl.when(pl.program_id(1) == 0)
    def _():
        tail_ref[...] = jnp.zeros_like(tail_ref)

    u = gc_ref[...].astype(F32) * hv_ref[...].astype(F32)
    full = jnp.concatenate([tail_ref[...], u], axis=0)
    u1 = pltpu.roll(full, 1, axis=0)[SUBLANES:]
    u2 = pltpu.roll(full, 2, axis=0)[SUBLANES:]
    tail_ref[...] = u[ct - SUBLANES:]
    w = cw_ref[...]
    y = u2 * w[0:1] + u1 * w[1:2] + u * w[2:3]
    y = gb_ref[...].astype(F32) * y
    gd = y.shape[1] // CONV_GROUPS
    for gi in range(CONV_GROUPS):
        sl = slice(gi * gd, (gi + 1) * gd)
        o_ref[:, sl] = _rms(y[:, sl], gn_ref[:, sl]).astype(o_ref.dtype)


def _conv_mixer(proj, conv_w, gn, bsz, length, ct=512):
    t = proj.shape[0]
    dc = conv_w.shape[1]
    ct = _pick(length, ct)
    nc = length // ct

    def col(j):
        return pl.BlockSpec((ct, dc), lambda b, c: (b * nc + c, j))

    return pl.pallas_call(
        _conv_kernel,
        grid=(bsz, nc),
        in_specs=[col(0), col(1), col(2),
                  pl.BlockSpec(conv_w.shape, lambda b, c: (0, 0)),
                  pl.BlockSpec((1, dc), lambda b, c: (0, 0))],
        out_specs=pl.BlockSpec((ct, dc), lambda b, c: (b * nc + c, 0)),
        out_shape=jax.ShapeDtypeStruct((t, dc), BF16),
        scratch_shapes=[pltpu.VMEM((SUBLANES, dc), F32)],
        compiler_params=_params("parallel", "arbitrary"),
        name="conv_mixer",
    )(proj, proj, proj, conv_w, gn.reshape(1, dc))


def _level_table(c):
    t = jnp.arange(c, dtype=jnp.int32)[:, None]
    s = jnp.arange(c, dtype=jnp.int32)[None, :]
    x = t ^ s
    lvl = jnp.zeros((c, c), jnp.int32)
    m = 1
    while m < c:
        lvl = jnp.where(x >= m, m, lvl)
        m *= 2
    return jnp.where(t > s, lvl, 0)


def _tiles(x):
    return [x[j * SUBLANES:(j + 1) * SUBLANES, :] for j in range(x.shape[0] // SUBLANES)]


def _bcast_row(x, r):
    return jnp.broadcast_to(x[r:r + 1, :], x.shape)


def _hgrn_chunk(q, z, v, g, lb, one_m_lb, gn, st, lvl):
    c, dk = q.shape
    nv = c // SUBLANES
    sub = lax.broadcasted_iota(jnp.int32, (SUBLANES, dk), 0)

    qf = _silu(q)
    k = one_m_lb * (1.0 / (1.0 + jnp.exp2(z * LOG2E)))
    fc = jnp.maximum(1.0 - k, MIN_FORGET)
    lf = jnp.log2(fc)

    qf_t, k_t, fc_t = _tiles(qf), _tiles(k), _tiles(fc)

    loc_t = []
    for x in _tiles(lf):
        for s in (1, 2, 4):
            x = x + jnp.where(sub >= s, pltpu.roll(x, s, axis=0), 0.0)
        loc_t.append(x)
    b_t = [loc_t[0]]
    for j in range(1, nv):
        b_t.append(loc_t[j] + _bcast_row(b_t[j - 1], SUBLANES - 1))
    b = jnp.concatenate(b_t, axis=0)
    b_last = b_t[nv - 1][SUBLANES - 1:SUBLANES, :]

    def level_operand(m):
        p_t = []
        if m == 1:
            odd = (sub & 1) == 1
            for j in range(nv):
                p_t.append(jnp.where(odd, qf_t[j] * fc_t[j], k_t[j]))
        elif m < SUBLANES:
            is_q = (sub & m) != 0
            for j in range(nv):
                x = loc_t[j]
                if m == 2:
                    ref = jnp.where(sub < 4, _bcast_row(x, 1), _bcast_row(x, 5))
                else:
                    ref = _bcast_row(x, 3)
                em = jnp.exp2(-jnp.abs(x - ref))
                p_t.append(jnp.where(is_q, qf_t[j], k_t[j]) * em)
        else:
            mv = m // SUBLANES
            for j0 in range(0, nv, 2 * mv):
                ref = _bcast_row(b_t[j0 + mv - 1], SUBLANES - 1)
                for j in range(j0, j0 + mv):
                    p_t.append(k_t[j] * jnp.exp2(ref - b_t[j]))
                for j in range(j0 + mv, j0 + 2 * mv):
                    p_t.append(qf_t[j] * jnp.exp2(b_t[j] - ref))
        return jnp.concatenate(p_t, axis=0).astype(BF16)

    scores = jnp.zeros((c, c), F32)
    m = 1
    while m < c:
        p = level_operand(m)
        scores = jnp.where(lvl == m, _dot_t(p, p), scores)
        m *= 2

    diag = jnp.sum(qf * k, axis=-1, keepdims=True)
    v_bf = v.astype(BF16)
    qe = (qf * jnp.exp2(b)).astype(BF16)
    o = (_dot_t(qe, st.astype(BF16))
         + jnp.dot(scores.astype(BF16), v_bf, preferred_element_type=F32) + diag * v)

    khat = (k * jnp.exp2(b_last - b)).astype(BF16)
    st_new = st * jnp.exp2(b_last) + lax.dot_general(
        v_bf, khat, (((0,), (0,)), ((), ())), preferred_element_type=F32)

    out = _rms(o, gn) * _silu(g)
    return out, st_new


def _hgrn_kernel(layer, c, q_ref, z_ref, v_ref, g_ref, lbl_ref, gn_ref, lvl_ref, o_ref, st_ref):
    ct = o_ref.shape[0]

    @pl.when(pl.program_id(2) == 0)
    def _():
        st_ref[...] = jnp.zeros_like(st_ref)

    lg = lbl_ref[...]
    ex = jnp.exp(lg - jnp.max(lg, axis=0, keepdims=True))
    p = ex / jnp.sum(ex, axis=0, keepdims=True)
    cum = jnp.sum(p[:layer + 1], axis=0, keepdims=True)
    lb = jnp.clip(cum - p[0:1], 0.0, 1.0 - 1e-4)
    one_m_lb = 1.0 - lb
    gn = gn_ref[...]
    lvl = lvl_ref[...]

    st = st_ref[...]
    for i in range(ct // c):
        rows = slice(i * c, (i + 1) * c)
        out, st = _hgrn_chunk(q_ref[rows, :].astype(F32), z_ref[rows, :].astype(F32),
                              v_ref[rows, :].astype(F32), g_ref[rows, :].astype(F32),
                              lb, one_m_lb, gn, st, lvl)
        o_ref[rows, :] = out.astype(o_ref.dtype)
    st_ref[...] = st


def _hgrn_mixer(proj, lb_logits, gn, layer, bsz, length, col0, ct=2048):
    t = proj.shape[0]
    dh = lb_logits.shape[1]
    dk = dh // HGRN_HEADS
    ct = _pick(length, ct)
    c = min(HGRN_CHUNK, ct)
    nc = length // ct
    depth = lb_logits.shape[0]

    def col(j):
        return pl.BlockSpec((ct, dk), lambda b, h, cc: (b * nc + cc, col0 + j * HGRN_HEADS + h))

    return pl.pallas_call(
        functools.partial(_hgrn_kernel, layer, c),
        grid=(bsz, HGRN_HEADS, nc),
        in_specs=[col(0), col(1), col(2), col(3),
                  pl.BlockSpec((depth, dk), lambda b, h, cc: (0, h)),
                  pl.BlockSpec((1, dk), lambda b, h, cc: (0, h)),
                  pl.BlockSpec((c, c), lambda b, h, cc: (0, 0))],
        out_specs=pl.BlockSpec((ct, dk), lambda b, h, cc: (b * nc + cc, h)),
        out_shape=jax.ShapeDtypeStruct((t, dh), BF16),
        scratch_shapes=[pltpu.VMEM((dk, dk), F32)],
        compiler_params=_params("parallel", "parallel", "arbitrary"),
        name="hgrn_mixer",
    )(proj, proj, proj, proj, lb_logits, gn.reshape(1, dh), _level_table(c))


def kernel(x, norm_mix, w_in, conv_w, gn_conv, lb_logits, gn_hgrn, w_out, norm_ffn,
           w_gate, w_up, w_down, norm_final):
    bsz, length, d = x.shape
    depth = w_in.shape[0]
    dc = conv_w.shape[-1]
    dh = lb_logits.shape[-1]
    t = bsz * length
    hgrn_col0 = (3 * dc) // (dh // HGRN_HEADS)

    xr = x.reshape(t, d)
    h = _norm(xr, norm_mix[0], BF16)
    for l in range(depth):
        proj = _matmul(h, w_in[l], BF16)
        conv_out = _conv_mixer(proj, conv_w[l], gn_conv[l], bsz, length)
        hgrn_out = _hgrn_mixer(proj, lb_logits, gn_hgrn[l], l, bsz, length, hgrn_col0)
        xr, h = _out_proj(conv_out, hgrn_out, w_out[l].astype(BF16), xr, norm_ffn[l])
        act = _glu(h, w_gate[l], w_up[l])
        last = l == depth - 1
        if last:
            y = _down_proj(act, w_down[l].astype(BF16), xr, norm_final, True)
        else:
            xr, h = _down_proj(act, w_down[l].astype(BF16), xr, norm_mix[l + 1], False)
    return y.reshape(bsz, length, d)
```

```python
import functools

import jax
import jax.numpy as jnp
from jax import lax
from jax.experimental import pallas as pl
from jax.experimental.pallas import tpu as pltpu

EPS = 1e-6
MIN_FORGET = 1e-30
LOG2E = 1.4426950408889634
CONV_GROUPS = 8
HGRN_HEADS = 8
LANES = 128
SUBLANES = 8
VMEM_LIMIT = 56 * 1024 * 1024
HGRN_CHUNK = 128

F32 = jnp.float32
BF16 = jnp.bfloat16


def _params(*sem):
    return pltpu.CompilerParams(dimension_semantics=sem, vmem_limit_bytes=VMEM_LIMIT)


def _pick(n, pref):
    t = min(pref, n)
    while n % t:
        t -= LANES
    return t


def _rms(xf, gain):
    ms = jnp.mean(xf * xf, axis=-1, keepdims=True)
    return xf * lax.rsqrt(ms + EPS) * gain


def _silu(x):
    return x * (1.0 / (1.0 + jnp.exp2(x * (-LOG2E))))


def _dot_t(a, b):
    return lax.dot_general(a, b, (((1,), (1,)), ((), ())), preferred_element_type=F32)


def _norm_kernel(x_ref, g_ref, o_ref):
    o_ref[...] = _rms(x_ref[...], g_ref[...]).astype(o_ref.dtype)


def _norm(x, gain, out_dtype, tm=512):
    t, d = x.shape
    tm = _pick(t, tm)
    return pl.pallas_call(
        _norm_kernel,
        grid=(t // tm,),
        in_specs=[pl.BlockSpec((tm, d), lambda i: (i, 0)),
                  pl.BlockSpec((1, d), lambda i: (0, 0))],
        out_specs=pl.BlockSpec((tm, d), lambda i: (i, 0)),
        out_shape=jax.ShapeDtypeStruct((t, d), out_dtype),
        compiler_params=_params("parallel"),
        name="rms_norm",
    )(x, gain.reshape(1, d))


def _mm_kernel(a_ref, w_ref, o_ref, wb_ref):
    @pl.when(pl.program_id(1) == 0)
    def _():
        wb_ref[...] = w_ref[...].astype(BF16)

    o_ref[...] = jnp.dot(a_ref[...], wb_ref[...],
                         preferred_element_type=F32).astype(o_ref.dtype)


def _wtile(k, tn, layer):
    return pl.BlockSpec((None, k, tn), lambda j, i: (layer, 0, j), pipeline_mode=pl.Buffered(1))


def _matmul(a, w, layer, out_dtype, tm=2048, tn=1024):
    t, k = a.shape
    n = w.shape[2]
    tm, tn = _pick(t, tm), _pick(n, tn)
    return pl.pallas_call(
        _mm_kernel,
        grid=(n // tn, t // tm),
        in_specs=[pl.BlockSpec((tm, k), lambda j, i: (i, 0)),
                  _wtile(k, tn, layer)],
        out_specs=pl.BlockSpec((tm, tn), lambda j, i: (i, j)),
        out_shape=jax.ShapeDtypeStruct((t, n), out_dtype),
        scratch_shapes=[pltpu.VMEM((k, tn), BF16)],
        compiler_params=_params("arbitrary", "arbitrary"),
        name="in_proj",
    )(a, w)


def _glu_kernel(a_ref, wg_ref, wu_ref, o_ref, wgb_ref, wub_ref):
    @pl.when(pl.program_id(1) == 0)
    def _():
        wgb_ref[...] = wg_ref[...].astype(BF16)
        wub_ref[...] = wu_ref[...].astype(BF16)

    a = a_ref[...]
    g = jnp.dot(a, wgb_ref[...], preferred_element_type=F32)
    u = jnp.dot(a, wub_ref[...], preferred_element_type=F32)
    o_ref[...] = (_silu(g) * u).astype(o_ref.dtype)


def _glu(a, wg, wu, layer, tm=2048, tn=512):
    t, k = a.shape
    n = wg.shape[2]
    tm, tn = _pick(t, tm), _pick(n, tn)
    return pl.pallas_call(
        _glu_kernel,
        grid=(n // tn, t // tm),
        in_specs=[pl.BlockSpec((tm, k), lambda j, i: (i, 0)),
                  _wtile(k, tn, layer),
                  _wtile(k, tn, layer)],
        out_specs=pl.BlockSpec((tm, tn), lambda j, i: (i, j)),
        out_shape=jax.ShapeDtypeStruct((t, n), BF16),
        scratch_shapes=[pltpu.VMEM((k, tn), BF16), pltpu.VMEM((k, tn), BF16)],
        compiler_params=_params("arbitrary", "arbitrary"),
        name="ffn_glu",
    )(a, wg, wu)


def _out_kernel(a1_ref, a2_ref, w1_ref, w2_ref, r_ref, g_ref, x_ref, h_ref):
    acc = jnp.dot(a1_ref[...], w1_ref[...], preferred_element_type=F32)
    acc = acc + jnp.dot(a2_ref[...], w2_ref[...], preferred_element_type=F32)
    xn = r_ref[...] + acc
    x_ref[...] = xn
    h_ref[...] = _rms(xn, g_ref[...]).astype(h_ref.dtype)


def _out_proj(a1, a2, w, layer, res, gain, tm=512):
    t, k1 = a1.shape
    k2 = a2.shape[1]
    d = w.shape[2]
    assert k1 == k2
    tm = _pick(t, tm)
    return pl.pallas_call(
        _out_kernel,
        grid=(t // tm,),
        in_specs=[pl.BlockSpec((tm, k1), lambda i: (i, 0)),
                  pl.BlockSpec((tm, k2), lambda i: (i, 0)),
                  pl.BlockSpec((None, k1, d), lambda i: (layer, 0, 0)),
                  pl.BlockSpec((None, k2, d), lambda i: (layer, 1, 0)),
                  pl.BlockSpec((tm, d), lambda i: (i, 0)),
                  pl.BlockSpec((1, d), lambda i: (0, 0))],
        out_specs=[pl.BlockSpec((tm, d), lambda i: (i, 0)),
                   pl.BlockSpec((tm, d), lambda i: (i, 0))],
        out_shape=[jax.ShapeDtypeStruct((t, d), F32),
                   jax.ShapeDtypeStruct((t, d), BF16)],
        compiler_params=_params("parallel"),
        name="out_proj",
    )(a1, a2, w, w, res, gain.reshape(1, d))


def _down_kernel(emit_x, a_ref, w_ref, r_ref, g_ref, *outs):
    xn = r_ref[...] + jnp.dot(a_ref[...], w_ref[...], preferred_element_type=F32)
    if emit_x:
        outs[0][...] = xn
    outs[-1][...] = _rms(xn, g_ref[...]).astype(outs[-1].dtype)


def _down_proj(a, w, layer, res, gain, last, tm=256):
    t, k = a.shape
    d = w.shape[2]
    tm = _pick(t, tm)
    row = pl.BlockSpec((tm, d), lambda i: (i, 0))
    if last:
        out_specs = row
        out_shape = jax.ShapeDtypeStruct((t, d), F32)
    else:
        out_specs = [row, row]
        out_shape = [jax.ShapeDtypeStruct((t, d), F32), jax.ShapeDtypeStruct((t, d), BF16)]
    return pl.pallas_call(
        functools.partial(_down_kernel, not last),
        grid=(t // tm,),
        in_specs=[pl.BlockSpec((tm, k), lambda i: (i, 0)),
                  pl.BlockSpec((None, k, d), lambda i: (layer, 0, 0), pipeline_mode=pl.Buffered(1)),
                  row,
                  pl.BlockSpec((1, d), lambda i: (0, 0))],
        out_specs=out_specs,
        out_shape=out_shape,
        compiler_params=_params("parallel"),
        name="down_proj",
    )(a, w, res, gain.reshape(1, d))


def _conv_kernel(gb_ref, gc_ref, hv_ref, cw_ref, gn_ref, o_ref, tail_ref):
    ct = o_ref.shape[0]

    @pl.when(pl.program_id(1) == 0)
    def _():
        tail_ref[...] = jnp.zeros_like(tail_ref)

    u = gc_ref[...].astype(F32) * hv_ref[...].astype(F32)
    full = jnp.concatenate([tail_ref[...], u], axis=0)
    u1 = pltpu.roll(full, 1, axis=0)[SUBLANES:]
    u2 = pltpu.roll(full, 2, axis=0)[SUBLANES:]
    tail_ref[...] = u[ct - SUBLANES:]
    w = cw_ref[...]
    y = u2 * w[0:1] + u1 * w[1:2] + u * w[2:3]
    y = gb_ref[...].astype(F32) * y
    gd = y.shape[1] // CONV_GROUPS
    for gi in range(CONV_GROUPS):
        sl = slice(gi * gd, (gi + 1) * gd)
        o_ref[:, sl] = _rms(y[:, sl], gn_ref[:, sl]).astype(o_ref.dtype)


def _conv_mixer(proj, conv_w, gn, bsz, length, ct=512):
    t = proj.shape[0]
    dc = conv_w.shape[1]
    ct = _pick(length, ct)
    nc = length // ct

    def col(j):
        return pl.BlockSpec((ct, dc), lambda b, c: (b * nc + c, j))

    return pl.pallas_call(
        _conv_kernel,
        grid=(bsz, nc),
        in_specs=[col(0), col(1), col(2),
                  pl.BlockSpec(conv_w.shape, lambda b, c: (0, 0)),
                  pl.BlockSpec((1, dc), lambda b, c: (0, 0))],
        out_specs=pl.BlockSpec((ct, dc), lambda b, c: (b * nc + c, 0)),
        out_shape=jax.ShapeDtypeStruct((t, dc), BF16),
        scratch_shapes=[pltpu.VMEM((SUBLANES, dc), F32)],
        compiler_params=_params("parallel", "arbitrary"),
        name="conv_mixer",
    )(proj, proj, proj, conv_w, gn.reshape(1, dc))


def _level_table(c):
    t = jnp.arange(c, dtype=jnp.int32)[:, None]
    s = jnp.arange(c, dtype=jnp.int32)[None, :]
    x = t ^ s
    lvl = jnp.zeros((c, c), jnp.int32)
    m = 1
    while m < c:
        lvl = jnp.where(x >= m, m, lvl)
        m *= 2
    return jnp.where(t > s, lvl, 0)


def _tiles(x):
    return [x[j * SUBLANES:(j + 1) * SUBLANES, :] for j in range(x.shape[0] // SUBLANES)]


def _bcast_row(x, r):
    return jnp.broadcast_to(x[r:r + 1, :], x.shape)


def _hgrn_chunk(q, z, v, g, lb, one_m_lb, gn, st, lvl):
    c, dk = q.shape
    nv = c // SUBLANES
    sub = lax.broadcasted_iota(jnp.int32, (SUBLANES, dk), 0)

    qf = _silu(q)
    k = one_m_lb * (1.0 / (1.0 + jnp.exp2(z * LOG2E)))
    fc = jnp.maximum(1.0 - k, MIN_FORGET)
    lf = jnp.log2(fc)

    qf_t, k_t, fc_t = _tiles(qf), _tiles(k), _tiles(fc)

    loc_t = []
    for x in _tiles(lf):
        for s in (1, 2, 4):
            x = x + jnp.where(sub >= s, pltpu.roll(x, s, axis=0), 0.0)
        loc_t.append(x)
    b_t = [loc_t[0]]
    for j in range(1, nv):
        b_t.append(loc_t[j] + _bcast_row(b_t[j - 1], SUBLANES - 1))
    b = jnp.concatenate(b_t, axis=0)
    b_last = b_t[nv - 1][SUBLANES - 1:SUBLANES, :]

    def level_operand(m):
        p_t = []
        if m == 1:
            odd = (sub & 1) == 1
            for j in range(nv):
                p_t.append(jnp.where(odd, qf_t[j] * fc_t[j], k_t[j]))
        elif m < SUBLANES:
            is_q = (sub & m) != 0
            for j in range(nv):
                x = loc_t[j]
                if m == 2:
                    ref = jnp.where(sub < 4, _bcast_row(x, 1), _bcast_row(x, 5))
                else:
                    ref = _bcast_row(x, 3)
                em = jnp.exp2(-jnp.abs(x - ref))
                p_t.append(jnp.where(is_q, qf_t[j], k_t[j]) * em)
        else:
            mv = m // SUBLANES
            for j0 in range(0, nv, 2 * mv):
                ref = _bcast_row(b_t[j0 + mv - 1], SUBLANES - 1)
                for j in range(j0, j0 + mv):
                    p_t.append(k_t[j] * jnp.exp2(ref - b_t[j]))
                for j in range(j0 + mv, j0 + 2 * mv):
                    p_t.append(qf_t[j] * jnp.exp2(b_t[j] - ref))
        return jnp.concatenate(p_t, axis=0).astype(BF16)

    scores = jnp.zeros((c, c), F32)
    m = 1
    while m < c:
        p = level_operand(m)
        scores = jnp.where(lvl == m, _dot_t(p, p), scores)
        m *= 2

    diag = jnp.sum(qf * k, axis=-1, keepdims=True)
    v_bf = v.astype(BF16)
    qe = (qf * jnp.exp2(b)).astype(BF16)
    o = (_dot_t(qe, st.astype(BF16))
         + jnp.dot(scores.astype(BF16), v_bf, preferred_element_type=F32) + diag * v)

    khat = (k * jnp.exp2(b_last - b)).astype(BF16)
    st_new = st * jnp.exp2(b_last) + lax.dot_general(
        v_bf, khat, (((0,), (0,)), ((), ())), preferred_element_type=F32)

    out = _rms(o, gn) * _silu(g)
    return out, st_new


def _hgrn_kernel(layer, c, q_ref, z_ref, v_ref, g_ref, lbl_ref, gn_ref, lvl_ref, o_ref, st_ref):
    ct = o_ref.shape[0]

    @pl.when(pl.program_id(2) == 0)
    def _():
        st_ref[...] = jnp.zeros_like(st_ref)

    lg = lbl_ref[...]
    ex = jnp.exp(lg - jnp.max(lg, axis=0, keepdims=True))
    p = ex / jnp.sum(ex, axis=0, keepdims=True)
    cum = jnp.sum(p[:layer + 1], axis=0, keepdims=True)
    lb = jnp.clip(cum - p[0:1], 0.0, 1.0 - 1e-4)
    one_m_lb = 1.0 - lb
    gn = gn_ref[...]
    lvl = lvl_ref[...]

    st = st_ref[...]
    for i in range(ct // c):
        rows = slice(i * c, (i + 1) * c)
        out, st = _hgrn_chunk(q_ref[rows, :].astype(F32), z_ref[rows, :].astype(F32),
                              v_ref[rows, :].astype(F32), g_ref[rows, :].astype(F32),
                              lb, one_m_lb, gn, st, lvl)
        o_ref[rows, :] = out.astype(o_ref.dtype)
    st_ref[...] = st


def _hgrn_mixer(proj, lb_logits, gn, layer, bsz, length, col0, ct=2048):
    t = proj.shape[0]
    dh = lb_logits.shape[1]
    dk = dh // HGRN_HEADS
    ct = _pick(length, ct)
    c = min(HGRN_CHUNK, ct)
    nc = length // ct
    depth = lb_logits.shape[0]

    def col(j):
        return pl.BlockSpec((ct, dk), lambda b, h, cc: (b * nc + cc, col0 + j * HGRN_HEADS + h))

    return pl.pallas_call(
        functools.partial(_hgrn_kernel, layer, c),
        grid=(bsz, HGRN_HEADS, nc),
        in_specs=[col(0), col(1), col(2), col(3),
                  pl.BlockSpec((depth, dk), lambda b, h, cc: (0, h)),
                  pl.BlockSpec((1, dk), lambda b, h, cc: (0, h)),
                  pl.BlockSpec((c, c), lambda b, h, cc: (0, 0))],
        out_specs=pl.BlockSpec((ct, dk), lambda b, h, cc: (b * nc + cc, h)),
        out_shape=jax.ShapeDtypeStruct((t, dh), BF16),
        scratch_shapes=[pltpu.VMEM((dk, dk), F32)],
        compiler_params=_params("parallel", "parallel", "arbitrary"),
        name="hgrn_mixer",
    )(proj, proj, proj, proj, lb_logits, gn.reshape(1, dh), _level_table(c))


def kernel(x, norm_mix, w_in, conv_w, gn_conv, lb_logits, gn_hgrn, w_out, norm_ffn,
           w_gate, w_up, w_down, norm_final):
    bsz, length, d = x.shape
    depth = w_in.shape[0]
    dc = conv_w.shape[-1]
    dh = lb_logits.shape[-1]
    t = bsz * length
    hgrn_col0 = (3 * dc) // (dh // HGRN_HEADS)

    xr = x.reshape(t, d)
    w_out_bf = w_out.astype(BF16)
    w_down_bf = w_down.astype(BF16)
    h = _norm(xr, norm_mix[0], BF16)
    for l in range(depth):
        proj = _matmul(h, w_in, l, BF16)
        conv_out = _conv_mixer(proj, conv_w[l], gn_conv[l], bsz, length)
        hgrn_out = _hgrn_mixer(proj, lb_logits, gn_hgrn[l], l, bsz, length, hgrn_col0)
        xr, h = _out_proj(conv_out, hgrn_out, w_out_bf, l, xr, norm_ffn[l])
        act = _glu(h, w_gate, w_up, l)
        last = l == depth - 1
        if last:
            y = _down_proj(act, w_down_bf, l, xr, norm_final, True)
        else:
            xr, h = _down_proj(act, w_down_bf, l, xr, norm_mix[l + 1], False)
    return y.reshape(bsz, length, d)
```

```python
import functools

import jax
import jax.numpy as jnp
from jax import lax
from jax.experimental import pallas as pl
from jax.experimental.pallas import tpu as pltpu

EPS = 1e-6
MIN_FORGET = 1e-30
LOG2E = 1.4426950408889634
CONV_GROUPS = 8
HGRN_HEADS = 8
LANES = 128
SUBLANES = 8
VMEM_LIMIT = 56 * 1024 * 1024
HGRN_CHUNK = 128
HGRN_ROW_GROUP = 512

F32 = jnp.float32
BF16 = jnp.bfloat16


def _params(*sem):
    return pltpu.CompilerParams(dimension_semantics=sem, vmem_limit_bytes=VMEM_LIMIT)


def _pick(n, pref):
    t = min(pref, n)
    while n % t:
        t -= LANES
    return t


def _rms(xf, gain):
    ms = jnp.mean(xf * xf, axis=-1, keepdims=True)
    return xf * lax.rsqrt(ms + EPS) * gain


def _silu(x):
    return x * (1.0 / (1.0 + jnp.exp2(x * (-LOG2E))))


def _dot_t(a, b):
    return lax.dot_general(a, b, (((1,), (1,)), ((), ())), preferred_element_type=F32)


def _wtile(k, tn, layer, col):
    return pl.BlockSpec((None, k, tn), lambda j, i: (layer, 0, col(j)),
                        pipeline_mode=pl.Buffered(1))


def _norm_kernel(x_ref, g_ref, o_ref):
    o_ref[...] = _rms(x_ref[...], g_ref[...]).astype(o_ref.dtype)


def _norm(x, gain, out_dtype, tm=512):
    t, d = x.shape
    tm = _pick(t, tm)
    return pl.pallas_call(
        _norm_kernel,
        grid=(t // tm,),
        in_specs=[pl.BlockSpec((tm, d), lambda i: (i, 0)),
                  pl.BlockSpec((1, d), lambda i: (0, 0))],
        out_specs=pl.BlockSpec((tm, d), lambda i: (i, 0)),
        out_shape=jax.ShapeDtypeStruct((t, d), out_dtype),
        compiler_params=_params("parallel"),
        name="rms_norm",
    )(x, gain.reshape(1, d))


def _glu_kernel(a_ref, wg_ref, wu_ref, o_ref, wgb_ref, wub_ref):
    @pl.when(pl.program_id(1) == 0)
    def _():
        wgb_ref[...] = wg_ref[...].astype(BF16)
        wub_ref[...] = wu_ref[...].astype(BF16)

    a = a_ref[...]
    g = jnp.dot(a, wgb_ref[...], preferred_element_type=F32)
    u = jnp.dot(a, wub_ref[...], preferred_element_type=F32)
    o_ref[...] = (_silu(g) * u).astype(o_ref.dtype)


def _glu(a, wg, wu, layer, tm=1024, tn=512):
    t, k = a.shape
    n = wg.shape[2]
    tm, tn = _pick(t, tm), _pick(n, tn)
    return pl.pallas_call(
        _glu_kernel,
        grid=(n // tn, t // tm),
        in_specs=[pl.BlockSpec((tm, k), lambda j, i: (i, 0)),
                  _wtile(k, tn, layer, lambda j: j),
                  _wtile(k, tn, layer, lambda j: j)],
        out_specs=pl.BlockSpec((tm, tn), lambda j, i: (i, j)),
        out_shape=jax.ShapeDtypeStruct((t, n), BF16),
        scratch_shapes=[pltpu.VMEM((k, tn), BF16), pltpu.VMEM((k, tn), BF16)],
        compiler_params=_params("arbitrary", "arbitrary"),
        name="ffn_glu",
    )(a, wg, wu)


def _out_kernel(a1_ref, a2_ref, w1_ref, w2_ref, r_ref, g_ref, x_ref, h_ref):
    acc = jnp.dot(a1_ref[...], w1_ref[...], preferred_element_type=F32)
    acc = acc + jnp.dot(a2_ref[...], w2_ref[...], preferred_element_type=F32)
    xn = r_ref[...] + acc
    x_ref[...] = xn
    h_ref[...] = _rms(xn, g_ref[...]).astype(h_ref.dtype)


def _out_proj(a1, a2, w, layer, res, gain, tm=512):
    t, k1 = a1.shape
    k2 = a2.shape[1]
    d = w.shape[2]
    assert k1 == k2
    tm = _pick(t, tm)
    return pl.pallas_call(
        _out_kernel,
        grid=(t // tm,),
        in_specs=[pl.BlockSpec((tm, k1), lambda i: (i, 0)),
                  pl.BlockSpec((tm, k2), lambda i: (i, 0)),
                  pl.BlockSpec((None, k1, d), lambda i: (layer, 0, 0)),
                  pl.BlockSpec((None, k2, d), lambda i: (layer, 1, 0)),
                  pl.BlockSpec((tm, d), lambda i: (i, 0)),
                  pl.BlockSpec((1, d), lambda i: (0, 0))],
        out_specs=[pl.BlockSpec((tm, d), lambda i: (i, 0)),
                   pl.BlockSpec((tm, d), lambda i: (i, 0))],
        out_shape=[jax.ShapeDtypeStruct((t, d), F32),
                   jax.ShapeDtypeStruct((t, d), BF16)],
        compiler_params=_params("parallel"),
        name="out_proj",
    )(a1, a2, w, w, res, gain.reshape(1, d))


def _down_kernel(emit_x, a_ref, w_ref, r_ref, g_ref, *outs):
    xn = r_ref[...] + jnp.dot(a_ref[...], w_ref[...], preferred_element_type=F32)
    if emit_x:
        outs[0][...] = xn
    outs[-1][...] = _rms(xn, g_ref[...]).astype(outs[-1].dtype)


def _down_proj(a, w, layer, res, gain, last, tm=256):
    t, k = a.shape
    d = w.shape[2]
    tm = _pick(t, tm)
    row = pl.BlockSpec((tm, d), lambda i: (i, 0))
    if last:
        out_specs = row
        out_shape = jax.ShapeDtypeStruct((t, d), F32)
    else:
        out_specs = [row, row]
        out_shape = [jax.ShapeDtypeStruct((t, d), F32), jax.ShapeDtypeStruct((t, d), BF16)]
    return pl.pallas_call(
        functools.partial(_down_kernel, not last),
        grid=(t // tm,),
        in_specs=[pl.BlockSpec((tm, k), lambda i: (i, 0)),
                  pl.BlockSpec((None, k, d), lambda i: (layer, 0, 0), pipeline_mode=pl.Buffered(1)),
                  row,
                  pl.BlockSpec((1, d), lambda i: (0, 0))],
        out_specs=out_specs,
        out_shape=out_shape,
        compiler_params=_params("parallel"),
        name="down_proj",
    )(a, w, res, gain.reshape(1, d))


def _conv_kernel(rt, a_ref, wb_in, wc_in, wh_in, cw_ref, gn_ref, o_ref, wb_ref):
    length, cw = o_ref.shape

    @pl.when(pl.program_id(1) == 0)
    def _():
        for gi, w_ref in enumerate((wb_in, wc_in, wh_in)):
            wb_ref[:, gi * cw:(gi + 1) * cw] = w_ref[...].astype(BF16)

    w = cw_ref[...]
    gd = LANES
    tail = jnp.zeros((SUBLANES, cw), F32)
    for r in range(length // rt):
        rows = slice(r * rt, (r + 1) * rt)
        pr = jnp.dot(a_ref[rows, :], wb_ref[...], preferred_element_type=F32)
        u = pr[:, cw:2 * cw] * pr[:, 2 * cw:]
        full = jnp.concatenate([tail, u], axis=0)
        u1 = pltpu.roll(full, 1, axis=0)[SUBLANES:]
        u2 = pltpu.roll(full, 2, axis=0)[SUBLANES:]
        tail = u[rt - SUBLANES:]
        y = pr[:, :cw] * (u2 * w[0:1] + u1 * w[1:2] + u * w[2:3])
        for gi in range(cw // gd):
            sl = slice(gi * gd, (gi + 1) * gd)
            o_ref[rows, sl] = _rms(y[:, sl], gn_ref[:, sl]).astype(o_ref.dtype)


def _conv_mixer(h, w_in, conv_w, gn, layer, bsz, length, cw=512, rt=512):
    t, d = h.shape
    dc = conv_w.shape[1]
    assert dc // CONV_GROUPS == LANES
    cw = _pick(dc, cw)
    rt = _pick(length, rt)
    nj = dc // cw
    return pl.pallas_call(
        functools.partial(_conv_kernel, rt),
        grid=(nj, bsz),
        in_specs=[pl.BlockSpec((length, d), lambda j, b: (b, 0)),
                  _wtile(d, cw, layer, lambda j: j),
                  _wtile(d, cw, layer, lambda j: nj + j),
                  _wtile(d, cw, layer, lambda j: 2 * nj + j),
                  pl.BlockSpec((conv_w.shape[0], cw), lambda j, b: (0, j)),
                  pl.BlockSpec((1, cw), lambda j, b: (0, j))],
        out_specs=pl.BlockSpec((length, cw), lambda j, b: (b, j)),
        out_shape=jax.ShapeDtypeStruct((t, dc), BF16),
        scratch_shapes=[pltpu.VMEM((d, 3 * cw), BF16)],
        compiler_params=_params("arbitrary", "arbitrary"),
        name="conv_mixer",
    )(h, w_in, w_in, w_in, conv_w, gn.reshape(1, dc))


def _level_table(c):
    t = jnp.arange(c, dtype=jnp.int32)[:, None]
    s = jnp.arange(c, dtype=jnp.int32)[None, :]
    x = t ^ s
    lvl = jnp.zeros((c, c), jnp.int32)
    m = 1
    while m < c:
        lvl = jnp.where(x >= m, m, lvl)
        m *= 2
    return jnp.where(t > s, lvl, 0)


def _tiles(x):
    return [x[j * SUBLANES:(j + 1) * SUBLANES, :] for j in range(x.shape[0] // SUBLANES)]


def _bcast_row(x, r):
    return jnp.broadcast_to(x[r:r + 1, :], x.shape)


def _hgrn_chunk(q, z, v, g, one_m_lb, gn, st, lvl):
    c, dk = q.shape
    nv = c // SUBLANES
    sub = lax.broadcasted_iota(jnp.int32, (SUBLANES, dk), 0)

    qf = _silu(q)
    k = one_m_lb * (1.0 / (1.0 + jnp.exp2(z * LOG2E)))
    fc = jnp.maximum(1.0 - k, MIN_FORGET)
    lf = jnp.log2(fc)

    qf_t, k_t, fc_t = _tiles(qf), _tiles(k), _tiles(fc)

    loc_t = []
    for x in _tiles(lf):
        for s in (1, 2, 4):
            x = x + jnp.where(sub >= s, pltpu.roll(x, s, axis=0), 0.0)
        loc_t.append(x)
    b_t = [loc_t[0]]
    for j in range(1, nv):
        b_t.append(loc_t[j] + _bcast_row(b_t[j - 1], SUBLANES - 1))
    b = jnp.concatenate(b_t, axis=0)
    b_last = b_t[nv - 1][SUBLANES - 1:SUBLANES, :]

    def level_operand(m):
        p_t = []
        if m == 1:
            odd = (sub & 1) == 1
            for j in range(nv):
                p_t.append(jnp.where(odd, qf_t[j] * fc_t[j], k_t[j]))
        elif m < SUBLANES:
            is_q = (sub & m) != 0
            for j in range(nv):
                x = loc_t[j]
                if m == 2:
                    ref = jnp.where(sub < 4, _bcast_row(x, 1), _bcast_row(x, 5))
                else:
                    ref = _bcast_row(x, 3)
                em = jnp.exp2(-jnp.abs(x - ref))
                p_t.append(jnp.where(is_q, qf_t[j], k_t[j]) * em)
        else:
            mv = m // SUBLANES
            for j0 in range(0, nv, 2 * mv):
                ref = _bcast_row(b_t[j0 + mv - 1], SUBLANES - 1)
                for j in range(j0, j0 + mv):
                    p_t.append(k_t[j] * jnp.exp2(ref - b_t[j]))
                for j in range(j0 + mv, j0 + 2 * mv):
                    p_t.append(qf_t[j] * jnp.exp2(b_t[j] - ref))
        return jnp.concatenate(p_t, axis=0).astype(BF16)

    scores = jnp.zeros((c, c), F32)
    m = 1
    while m < c:
        p = level_operand(m)
        scores = jnp.where(lvl == m, _dot_t(p, p), scores)
        m *= 2

    diag = jnp.sum(qf * k, axis=-1, keepdims=True)
    v_bf = v.astype(BF16)
    qe = (qf * jnp.exp2(b)).astype(BF16)
    o = (_dot_t(qe, st.astype(BF16))
         + jnp.dot(scores.astype(BF16), v_bf, preferred_element_type=F32) + diag * v)

    khat = (k * jnp.exp2(b_last - b)).astype(BF16)
    st_new = st * jnp.exp2(b_last) + lax.dot_general(
        v_bf, khat, (((0,), (0,)), ((), ())), preferred_element_type=F32)

    out = _rms(o, gn) * _silu(g)
    return out, st_new


def _hgrn_kernel(layer, c, rg, a_ref, wq_in, wz_in, wv_in, wg_in, lbl_ref, gn_ref, lvl_ref,
                 o_ref, wb_ref):
    length, dk = o_ref.shape

    @pl.when(pl.program_id(1) == 0)
    def _():
        for gi, w_ref in enumerate((wq_in, wz_in, wv_in, wg_in)):
            wb_ref[:, gi * dk:(gi + 1) * dk] = w_ref[...].astype(BF16)

    lg = lbl_ref[...]
    ex = jnp.exp(lg - jnp.max(lg, axis=0, keepdims=True))
    p = ex / jnp.sum(ex, axis=0, keepdims=True)
    cum = jnp.sum(p[:layer + 1], axis=0, keepdims=True)
    lb = jnp.clip(cum - p[0:1], 0.0, 1.0 - 1e-4)
    one_m_lb = 1.0 - lb
    gn = gn_ref[...]
    lvl = lvl_ref[...]

    def project(r):
        return jnp.dot(a_ref[r * rg:(r + 1) * rg, :], wb_ref[...], preferred_element_type=F32)

    st = jnp.zeros((dk, dk), F32)
    n_groups = length // rg
    nxt = project(0)
    for r in range(n_groups):
        pr = nxt
        if r + 1 < n_groups:
            nxt = project(r + 1)
        for i in range(rg // c):
            p = pr[i * c:(i + 1) * c, :]
            out, st = _hgrn_chunk(p[:, :dk], p[:, dk:2 * dk], p[:, 2 * dk:3 * dk], p[:, 3 * dk:],
                                  one_m_lb, gn, st, lvl)
            o_ref[r * rg + i * c:r * rg + (i + 1) * c, :] = out.astype(o_ref.dtype)


def _hgrn_mixer(h, w_in, lb_logits, gn, layer, bsz, length, col0):
    t, d = h.shape
    depth, dh = lb_logits.shape
    dk = dh // HGRN_HEADS
    c = min(HGRN_CHUNK, length)

    def wcol(gi):
        return _wtile(d, dk, layer, lambda hh: col0 + gi * HGRN_HEADS + hh)

    return pl.pallas_call(
        functools.partial(_hgrn_kernel, layer, c, min(HGRN_ROW_GROUP, length)),
        grid=(HGRN_HEADS, bsz),
        in_specs=[pl.BlockSpec((length, d), lambda hh, b: (b, 0)),
                  wcol(0), wcol(1), wcol(2), wcol(3),
                  pl.BlockSpec((depth, dk), lambda hh, b: (0, hh)),
                  pl.BlockSpec((1, dk), lambda hh, b: (0, hh)),
                  pl.BlockSpec((c, c), lambda hh, b: (0, 0))],
        out_specs=pl.BlockSpec((length, dk), lambda hh, b: (b, hh)),
        out_shape=jax.ShapeDtypeStruct((t, dh), BF16),
        scratch_shapes=[pltpu.VMEM((d, 4 * dk), BF16)],
        compiler_params=_params("arbitrary", "arbitrary"),
        name="hgrn_mixer",
    )(h, w_in, w_in, w_in, w_in, lb_logits, gn.reshape(1, dh), _level_table(c))


def kernel(x, norm_mix, w_in, conv_w, gn_conv, lb_logits, gn_hgrn, w_out, norm_ffn,
           w_gate, w_up, w_down, norm_final):
    bsz, length, d = x.shape
    depth = w_in.shape[0]
    dc = conv_w.shape[-1]
    dh = lb_logits.shape[-1]
    t = bsz * length
    hgrn_col0 = (3 * dc) // (dh // HGRN_HEADS)

    xr = x.reshape(t, d)
    w_out_bf = w_out.astype(BF16)
    w_down_bf = w_down.astype(BF16)
    h = _norm(xr, norm_mix[0], BF16)
    for l in range(depth):
        conv_out = _conv_mixer(h, w_in, conv_w[l], gn_conv[l], l, bsz, length)
        hgrn_out = _hgrn_mixer(h, w_in, lb_logits, gn_hgrn[l], l, bsz, length, hgrn_col0)
        xr, h = _out_proj(conv_out, hgrn_out, w_out_bf, l, xr, norm_ffn[l])
        act = _glu(h, w_gate, w_up, l)
        last = l == depth - 1
        if last:
            y = _down_proj(act, w_down_bf, l, xr, norm_final, True)
        else:
            xr, h = _down_proj(act, w_down_bf, l, xr, norm_mix[l + 1], False)
    return y.reshape(bsz, length, d)
```

```python
import functools

import jax
import jax.numpy as jnp
from jax import lax
from jax.experimental import pallas as pl
from jax.experimental.pallas import tpu as pltpu

EPS = 1e-6
MIN_FORGET = 1e-30
LOG2E = 1.4426950408889634
CONV_GROUPS = 8
HGRN_HEADS = 8
LANES = 128
SUBLANES = 8
VMEM_LIMIT = 56 * 1024 * 1024
HGRN_CHUNK = 128
HGRN_ROW_GROUP = 512
HGRN_HEADS_PER_STEP = 1

F32 = jnp.float32
BF16 = jnp.bfloat16


def _params(*sem):
    return pltpu.CompilerParams(dimension_semantics=sem, vmem_limit_bytes=VMEM_LIMIT)


def _pick(n, pref):
    t = min(pref, n)
    while n % t:
        t -= LANES
    return t


def _rms(xf, gain):
    ms = jnp.mean(xf * xf, axis=-1, keepdims=True)
    return xf * lax.rsqrt(ms + EPS) * gain


def _silu(x):
    return x * (1.0 / (1.0 + jnp.exp2(x * (-LOG2E))))


def _dot_t(a, b):
    return lax.dot_general(a, b, (((1,), (1,)), ((), ())), preferred_element_type=F32)


def _wtile(k, tn, layer, col, buffers=1):
    return pl.BlockSpec((None, k, tn), lambda j, i: (layer, 0, col(j)),
                        pipeline_mode=pl.Buffered(buffers))


def _norm_kernel(x_ref, g_ref, o_ref):
    o_ref[...] = _rms(x_ref[...], g_ref[...]).astype(o_ref.dtype)


def _norm(x, gain, out_dtype, tm=512):
    t, d = x.shape
    tm = _pick(t, tm)
    return pl.pallas_call(
        _norm_kernel,
        grid=(t // tm,),
        in_specs=[pl.BlockSpec((tm, d), lambda i: (i, 0)),
                  pl.BlockSpec((1, d), lambda i: (0, 0))],
        out_specs=pl.BlockSpec((tm, d), lambda i: (i, 0)),
        out_shape=jax.ShapeDtypeStruct((t, d), out_dtype),
        compiler_params=_params("parallel"),
        name="rms_norm",
    )(x, gain.reshape(1, d))


def _glu_kernel(a_ref, wg_ref, wu_ref, o_ref, wgb_ref, wub_ref):
    @pl.when(pl.program_id(1) == 0)
    def _():
        wgb_ref[...] = wg_ref[...].astype(BF16)
        wub_ref[...] = wu_ref[...].astype(BF16)

    a = a_ref[...]
    g = jnp.dot(a, wgb_ref[...], preferred_element_type=F32)
    u = jnp.dot(a, wub_ref[...], preferred_element_type=F32)
    o_ref[...] = (_silu(g) * u).astype(o_ref.dtype)


def _glu(a, wg, wu, layer, tm=1024, tn=512):
    t, k = a.shape
    n = wg.shape[2]
    tm, tn = _pick(t, tm), _pick(n, tn)
    return pl.pallas_call(
        _glu_kernel,
        grid=(n // tn, t // tm),
        in_specs=[pl.BlockSpec((tm, k), lambda j, i: (i, 0)),
                  _wtile(k, tn, layer, lambda j: j, buffers=2),
                  _wtile(k, tn, layer, lambda j: j, buffers=2)],
        out_specs=pl.BlockSpec((tm, tn), lambda j, i: (i, j)),
        out_shape=jax.ShapeDtypeStruct((t, n), BF16),
        scratch_shapes=[pltpu.VMEM((k, tn), BF16), pltpu.VMEM((k, tn), BF16)],
        compiler_params=_params("arbitrary", "arbitrary"),
        name="ffn_glu",
    )(a, wg, wu)


def _out_kernel(a1_ref, a2_ref, w1_ref, w2_ref, r_ref, g_ref, x_ref, h_ref):
    acc = jnp.dot(a1_ref[...], w1_ref[...], preferred_element_type=F32)
    acc = acc + jnp.dot(a2_ref[...], w2_ref[...], preferred_element_type=F32)
    xn = r_ref[...] + acc
    x_ref[...] = xn
    h_ref[...] = _rms(xn, g_ref[...]).astype(h_ref.dtype)


def _out_proj(a1, a2, w, layer, res, gain, tm=512):
    t, k1 = a1.shape
    k2 = a2.shape[1]
    d = w.shape[2]
    assert k1 == k2
    tm = _pick(t, tm)
    return pl.pallas_call(
        _out_kernel,
        grid=(t // tm,),
        in_specs=[pl.BlockSpec((tm, k1), lambda i: (i, 0)),
                  pl.BlockSpec((tm, k2), lambda i: (i, 0)),
                  pl.BlockSpec((None, k1, d), lambda i: (layer, 0, 0)),
                  pl.BlockSpec((None, k2, d), lambda i: (layer, 1, 0)),
                  pl.BlockSpec((tm, d), lambda i: (i, 0)),
                  pl.BlockSpec((1, d), lambda i: (0, 0))],
        out_specs=[pl.BlockSpec((tm, d), lambda i: (i, 0)),
                   pl.BlockSpec((tm, d), lambda i: (i, 0))],
        out_shape=[jax.ShapeDtypeStruct((t, d), F32),
                   jax.ShapeDtypeStruct((t, d), BF16)],
        compiler_params=_params("parallel"),
        name="out_proj",
    )(a1, a2, w, w, res, gain.reshape(1, d))


def _down_kernel(emit_x, a_ref, w_ref, r_ref, g_ref, *outs):
    xn = r_ref[...] + jnp.dot(a_ref[...], w_ref[...], preferred_element_type=F32)
    if emit_x:
        outs[0][...] = xn
    outs[-1][...] = _rms(xn, g_ref[...]).astype(outs[-1].dtype)


def _down_proj(a, w, layer, res, gain, last, tm=256):
    t, k = a.shape
    d = w.shape[2]
    tm = _pick(t, tm)
    row = pl.BlockSpec((tm, d), lambda i: (i, 0))
    if last:
        out_specs = row
        out_shape = jax.ShapeDtypeStruct((t, d), F32)
    else:
        out_specs = [row, row]
        out_shape = [jax.ShapeDtypeStruct((t, d), F32), jax.ShapeDtypeStruct((t, d), BF16)]
    return pl.pallas_call(
        functools.partial(_down_kernel, not last),
        grid=(t // tm,),
        in_specs=[pl.BlockSpec((tm, k), lambda i: (i, 0)),
                  pl.BlockSpec((None, k, d), lambda i: (layer, 0, 0), pipeline_mode=pl.Buffered(1)),
                  row,
                  pl.BlockSpec((1, d), lambda i: (0, 0))],
        out_specs=out_specs,
        out_shape=out_shape,
        compiler_params=_params("parallel"),
        name="down_proj",
    )(a, w, res, gain.reshape(1, d))


def _conv_kernel(rt, a_ref, wb_in, wc_in, wh_in, cw_ref, gn_ref, o_ref, wb_ref):
    length, cw = o_ref.shape

    @pl.when(pl.program_id(1) == 0)
    def _():
        for gi, w_ref in enumerate((wb_in, wc_in, wh_in)):
            wb_ref[:, gi * cw:(gi + 1) * cw] = w_ref[...].astype(BF16)

    w = cw_ref[...]
    gd = LANES
    tail = jnp.zeros((SUBLANES, cw), F32)
    for r in range(length // rt):
        rows = slice(r * rt, (r + 1) * rt)
        pr = jnp.dot(a_ref[rows, :], wb_ref[...], preferred_element_type=F32)
        u = pr[:, cw:2 * cw] * pr[:, 2 * cw:]
        full = jnp.concatenate([tail, u], axis=0)
        u1 = pltpu.roll(full, 1, axis=0)[SUBLANES:]
        u2 = pltpu.roll(full, 2, axis=0)[SUBLANES:]
        tail = u[rt - SUBLANES:]
        y = pr[:, :cw] * (u2 * w[0:1] + u1 * w[1:2] + u * w[2:3])
        for gi in range(cw // gd):
            sl = slice(gi * gd, (gi + 1) * gd)
            o_ref[rows, sl] = _rms(y[:, sl], gn_ref[:, sl]).astype(o_ref.dtype)


def _conv_mixer(h, w_in, conv_w, gn, layer, bsz, length, cw=512, rt=512):
    t, d = h.shape
    dc = conv_w.shape[1]
    assert dc // CONV_GROUPS == LANES
    cw = _pick(dc, cw)
    rt = _pick(length, rt)
    nj = dc // cw
    return pl.pallas_call(
        functools.partial(_conv_kernel, rt),
        grid=(nj, bsz),
        in_specs=[pl.BlockSpec((length, d), lambda j, b: (b, 0)),
                  _wtile(d, cw, layer, lambda j: j),
                  _wtile(d, cw, layer, lambda j: nj + j),
                  _wtile(d, cw, layer, lambda j: 2 * nj + j),
                  pl.BlockSpec((conv_w.shape[0], cw), lambda j, b: (0, j)),
                  pl.BlockSpec((1, cw), lambda j, b: (0, j))],
        out_specs=pl.BlockSpec((length, cw), lambda j, b: (b, j)),
        out_shape=jax.ShapeDtypeStruct((t, dc), BF16),
        scratch_shapes=[pltpu.VMEM((d, 3 * cw), BF16)],
        compiler_params=_params("arbitrary", "arbitrary"),
        name="conv_mixer",
    )(h, w_in, w_in, w_in, conv_w, gn.reshape(1, dc))


def _level_table(c):
    t = jnp.arange(c, dtype=jnp.int32)[:, None]
    s = jnp.arange(c, dtype=jnp.int32)[None, :]
    x = t ^ s
    lvl = jnp.zeros((c, c), jnp.int32)
    m = 1
    while m < c:
        lvl = jnp.where(x >= m, m, lvl)
        m *= 2
    return jnp.where(t > s, lvl, 0)


def _tiles(x):
    return [x[j * SUBLANES:(j + 1) * SUBLANES, :] for j in range(x.shape[0] // SUBLANES)]


def _bcast_row(x, r):
    return jnp.broadcast_to(x[r:r + 1, :], x.shape)


def _hgrn_chunk(q, z, v, g, one_m_lb, gn, st, lvl):
    c, dk = q.shape
    nv = c // SUBLANES
    sub = lax.broadcasted_iota(jnp.int32, (SUBLANES, dk), 0)

    qf = _silu(q)
    k = one_m_lb * (1.0 / (1.0 + jnp.exp2(z * LOG2E)))
    fc = jnp.maximum(1.0 - k, MIN_FORGET)
    lf = jnp.log2(fc)

    qf_t, k_t, fc_t = _tiles(qf), _tiles(k), _tiles(fc)

    loc_t = []
    for x in _tiles(lf):
        for s in (1, 2, 4):
            x = x + jnp.where(sub >= s, pltpu.roll(x, s, axis=0), 0.0)
        loc_t.append(x)
    b_t = [loc_t[0]]
    for j in range(1, nv):
        b_t.append(loc_t[j] + _bcast_row(b_t[j - 1], SUBLANES - 1))
    b = jnp.concatenate(b_t, axis=0)
    b_last = b_t[nv - 1][SUBLANES - 1:SUBLANES, :]

    def level_operand(m):
        p_t = []
        if m == 1:
            odd = (sub & 1) == 1
            for j in range(nv):
                p_t.append(jnp.where(odd, qf_t[j] * fc_t[j], k_t[j]))
        elif m < SUBLANES:
            is_q = (sub & m) != 0
            for j in range(nv):
                x = loc_t[j]
                if m == 2:
                    ref = jnp.where(sub < 4, _bcast_row(x, 1), _bcast_row(x, 5))
                else:
                    ref = _bcast_row(x, 3)
                em = jnp.exp2(-jnp.abs(x - ref))
                p_t.append(jnp.where(is_q, qf_t[j], k_t[j]) * em)
        else:
            mv = m // SUBLANES
            for j0 in range(0, nv, 2 * mv):
                ref = _bcast_row(b_t[j0 + mv - 1], SUBLANES - 1)
                for j in range(j0, j0 + mv):
                    p_t.append(k_t[j] * jnp.exp2(ref - b_t[j]))
                for j in range(j0 + mv, j0 + 2 * mv):
                    p_t.append(qf_t[j] * jnp.exp2(b_t[j] - ref))
        return jnp.concatenate(p_t, axis=0).astype(BF16)

    zero = jnp.zeros((c, dk), BF16)
    levels = [1 << i for i in range(c.bit_length() - 1)]
    scores = jnp.zeros((c, c), F32)
    for i in range(0, len(levels), 2):
        ms = levels[i:i + 2]
        ps = [level_operand(m) for m in ms]
        if len(ps) == 2:
            lhs = jnp.concatenate(ps, axis=1)
            rhs = jnp.concatenate([jnp.concatenate([ps[0], zero], axis=1),
                                   jnp.concatenate([zero, ps[1]], axis=1)], axis=0)
            sp = _dot_t(lhs, rhs)
        else:
            sp = _dot_t(ps[0], ps[0])
        for j, m in enumerate(ms):
            scores = jnp.where(lvl == m, sp[:, j * c:(j + 1) * c], scores)

    diag = jnp.sum(qf * k, axis=-1, keepdims=True)
    qe = (qf * jnp.exp2(b)).astype(BF16)
    khat_t = (k * jnp.exp2(b_last - b)).astype(BF16).T

    lhs = jnp.concatenate([jnp.concatenate([qe, scores.astype(BF16)], axis=1),
                           jnp.concatenate([jnp.zeros((dk, dk), BF16), khat_t], axis=1)], axis=0)
    rhs = jnp.concatenate([st.astype(BF16), v.astype(BF16)], axis=0)
    res = jnp.dot(lhs, rhs, preferred_element_type=F32)
    o = res[:c] + diag * v
    decay = jnp.transpose(jnp.broadcast_to(jnp.exp2(b_last), (dk, dk)))
    st_new = st * decay + res[c:]

    out = _rms(o, gn) * _silu(g)
    return out, st_new


def _hgrn_kernel(layer, c, rg, nh, a_ref, wq_in, wz_in, wv_in, wg_in, lbl_ref, gn_ref, lvl_ref,
                 o_ref, wb_ref):
    length, wd = o_ref.shape
    dk = wd // nh

    @pl.when(pl.program_id(1) == 0)
    def _():
        for gi, w_ref in enumerate((wq_in, wz_in, wv_in, wg_in)):
            wb_ref[:, gi * wd:(gi + 1) * wd] = w_ref[...].astype(BF16)

    lg = lbl_ref[...]
    ex = jnp.exp(lg - jnp.max(lg, axis=0, keepdims=True))
    p = ex / jnp.sum(ex, axis=0, keepdims=True)
    cum = jnp.sum(p[:layer + 1], axis=0, keepdims=True)
    lb = jnp.clip(cum - p[0:1], 0.0, 1.0 - 1e-4)
    one_m_lb = 1.0 - lb
    gn = gn_ref[...]
    lvl = lvl_ref[...]

    def project(r, half):
        cs = slice(half * 2 * wd, (half + 1) * 2 * wd)
        return jnp.dot(a_ref[r * rg:(r + 1) * rg, :], wb_ref[:, cs], preferred_element_type=F32)

    sts = [jnp.zeros((dk, dk), F32) for _ in range(nh)]
    n_groups = length // rg
    nxt = [project(0, 0), project(0, 1)]
    for r in range(n_groups):
        pr = nxt
        nxt = []
        for i in range(rg // c):
            rows = slice(i * c, (i + 1) * c)
            for e in range(nh):
                ls = slice(e * dk, (e + 1) * dk)
                ls2 = slice(wd + e * dk, wd + (e + 1) * dk)
                out, sts[e] = _hgrn_chunk(pr[0][rows, ls], pr[0][rows, ls2], pr[1][rows, ls],
                                          pr[1][rows, ls2], one_m_lb[:, ls], gn[:, ls], sts[e], lvl)
                o_ref[r * rg + i * c:r * rg + (i + 1) * c, ls] = out.astype(o_ref.dtype)
            if r + 1 < n_groups and i < 2:
                nxt.append(project(r + 1, i))


def _hgrn_mixer(h, w_in, lb_logits, gn, layer, bsz, length, col0):
    t, d = h.shape
    depth, dh = lb_logits.shape
    dk = dh // HGRN_HEADS
    nh = HGRN_HEADS_PER_STEP
    wd = nh * dk
    c = min(HGRN_CHUNK, length)
    assert HGRN_HEADS % nh == 0 and col0 % nh == 0

    def wcol(gi):
        return _wtile(d, wd, layer, lambda hp: (col0 + gi * HGRN_HEADS) // nh + hp)

    return pl.pallas_call(
        functools.partial(_hgrn_kernel, layer, c, min(HGRN_ROW_GROUP, length), nh),
        grid=(HGRN_HEADS // nh, bsz),
        in_specs=[pl.BlockSpec((length, d), lambda hp, b: (b, 0)),
                  wcol(0), wcol(1), wcol(2), wcol(3),
                  pl.BlockSpec((depth, wd), lambda hp, b: (0, hp)),
                  pl.BlockSpec((1, wd), lambda hp, b: (0, hp)),
                  pl.BlockSpec((c, c), lambda hp, b: (0, 0))],
        out_specs=pl.BlockSpec((length, wd), lambda hp, b: (b, hp)),
        out_shape=jax.ShapeDtypeStruct((t, dh), BF16),
        scratch_shapes=[pltpu.VMEM((d, 4 * wd), BF16)],
        compiler_params=_params("arbitrary", "arbitrary"),
        name="hgrn_mixer",
    )(h, w_in, w_in, w_in, w_in, lb_logits, gn.reshape(1, dh), _level_table(c))


def kernel(x, norm_mix, w_in, conv_w, gn_conv, lb_logits, gn_hgrn, w_out, norm_ffn,
           w_gate, w_up, w_down, norm_final):
    bsz, length, d = x.shape
    depth = w_in.shape[0]
    dc = conv_w.shape[-1]
    dh = lb_logits.shape[-1]
    t = bsz * length
    hgrn_col0 = (3 * dc) // (dh // HGRN_HEADS)

    xr = x.reshape(t, d)
    w_out_bf = w_out.astype(BF16)
    w_down_bf = w_down.astype(BF16)
    h = _norm(xr, norm_mix[0], BF16)
    for l in range(depth):
        conv_out = _conv_mixer(h, w_in, conv_w[l], gn_conv[l], l, bsz, length)
        hgrn_out = _hgrn_mixer(h, w_in, lb_logits, gn_hgrn[l], l, bsz, length, hgrn_col0)
        xr, h = _out_proj(conv_out, hgrn_out, w_out_bf, l, xr, norm_ffn[l])
        act = _glu(h, w_gate, w_up, l)
        last = l == depth - 1
        if last:
            y = _down_proj(act, w_down_bf, l, xr, norm_final, True)
        else:
            xr, h = _down_proj(act, w_down_bf, l, xr, norm_mix[l + 1], False)
    return y.reshape(bsz, length, d)
```

```python
import functools

import jax
import jax.numpy as jnp
from jax import lax
from jax.experimental import pallas as pl
from jax.experimental.pallas import tpu as pltpu

EPS = 1e-6
MIN_FORGET = 1e-30
LOG2E = 1.4426950408889634
CONV_GROUPS = 8
HGRN_HEADS = 8
LANES = 128
SUBLANES = 8
VMEM_LIMIT = 56 * 1024 * 1024
HGRN_CHUNK = 128
HGRN_ROW_GROUP = 512
HGRN_HEADS_PER_STEP = 2
HGRN_SKEW = 1

F32 = jnp.float32
BF16 = jnp.bfloat16


def _params(*sem):
    return pltpu.CompilerParams(dimension_semantics=sem, vmem_limit_bytes=VMEM_LIMIT)


def _pick(n, pref):
    t = min(pref, n)
    while n % t:
        t -= LANES
    return t


def _rms(xf, gain):
    ms = jnp.mean(xf * xf, axis=-1, keepdims=True)
    return xf * lax.rsqrt(ms + EPS) * gain


def _silu(x):
    return x * (1.0 / (1.0 + jnp.exp2(x * (-LOG2E))))


def _dot_t(a, b):
    return lax.dot_general(a, b, (((1,), (1,)), ((), ())), preferred_element_type=F32)


def _wtile(k, tn, layer, col, buffers=1):
    return pl.BlockSpec((None, k, tn), lambda j, i: (layer, 0, col(j)),
                        pipeline_mode=pl.Buffered(buffers))


def _norm_kernel(x_ref, g_ref, o_ref):
    o_ref[...] = _rms(x_ref[...], g_ref[...]).astype(o_ref.dtype)


def _norm(x, gain, out_dtype, tm=512):
    t, d = x.shape
    tm = _pick(t, tm)
    return pl.pallas_call(
        _norm_kernel,
        grid=(t // tm,),
        in_specs=[pl.BlockSpec((tm, d), lambda i: (i, 0)),
                  pl.BlockSpec((1, d), lambda i: (0, 0))],
        out_specs=pl.BlockSpec((tm, d), lambda i: (i, 0)),
        out_shape=jax.ShapeDtypeStruct((t, d), out_dtype),
        compiler_params=_params("parallel"),
        name="rms_norm",
    )(x, gain.reshape(1, d))


def _glu_kernel(a_ref, wg_ref, wu_ref, o_ref, wgb_ref, wub_ref):
    @pl.when(pl.program_id(1) == 0)
    def _():
        wgb_ref[...] = wg_ref[...].astype(BF16)
        wub_ref[...] = wu_ref[...].astype(BF16)

    a = a_ref[...]
    g = jnp.dot(a, wgb_ref[...], preferred_element_type=F32)
    u = jnp.dot(a, wub_ref[...], preferred_element_type=F32)
    o_ref[...] = (_silu(g) * u).astype(o_ref.dtype)


def _glu(a, wg, wu, layer, tm=1024, tn=512):
    t, k = a.shape
    n = wg.shape[2]
    tm, tn = _pick(t, tm), _pick(n, tn)
    return pl.pallas_call(
        _glu_kernel,
        grid=(n // tn, t // tm),
        in_specs=[pl.BlockSpec((tm, k), lambda j, i: (i, 0)),
                  _wtile(k, tn, layer, lambda j: j, buffers=2),
                  _wtile(k, tn, layer, lambda j: j, buffers=2)],
        out_specs=pl.BlockSpec((tm, tn), lambda j, i: (i, j)),
        out_shape=jax.ShapeDtypeStruct((t, n), BF16),
        scratch_shapes=[pltpu.VMEM((k, tn), BF16), pltpu.VMEM((k, tn), BF16)],
        compiler_params=_params("arbitrary", "arbitrary"),
        name="ffn_glu",
    )(a, wg, wu)


def _out_kernel(a1_ref, a2_ref, w1_ref, w2_ref, r_ref, g_ref, x_ref, h_ref):
    acc = jnp.dot(a1_ref[...], w1_ref[...], preferred_element_type=F32)
    acc = acc + jnp.dot(a2_ref[...], w2_ref[...], preferred_element_type=F32)
    xn = r_ref[...] + acc
    x_ref[...] = xn
    h_ref[...] = _rms(xn, g_ref[...]).astype(h_ref.dtype)


def _out_proj(a1, a2, w, layer, res, gain, tm=512):
    t, k1 = a1.shape
    k2 = a2.shape[1]
    d = w.shape[2]
    assert k1 == k2
    tm = _pick(t, tm)
    return pl.pallas_call(
        _out_kernel,
        grid=(t // tm,),
        in_specs=[pl.BlockSpec((tm, k1), lambda i: (i, 0)),
                  pl.BlockSpec((tm, k2), lambda i: (i, 0)),
                  pl.BlockSpec((None, k1, d), lambda i: (layer, 0, 0)),
                  pl.BlockSpec((None, k2, d), lambda i: (layer, 1, 0)),
                  pl.BlockSpec((tm, d), lambda i: (i, 0)),
                  pl.BlockSpec((1, d), lambda i: (0, 0))],
        out_specs=[pl.BlockSpec((tm, d), lambda i: (i, 0)),
                   pl.BlockSpec((tm, d), lambda i: (i, 0))],
        out_shape=[jax.ShapeDtypeStruct((t, d), F32),
                   jax.ShapeDtypeStruct((t, d), BF16)],
        compiler_params=_params("parallel"),
        name="out_proj",
    )(a1, a2, w, w, res, gain.reshape(1, d))


def _down_kernel(emit_x, a_ref, w_ref, r_ref, g_ref, *outs):
    xn = r_ref[...] + jnp.dot(a_ref[...], w_ref[...], preferred_element_type=F32)
    if emit_x:
        outs[0][...] = xn
    outs[-1][...] = _rms(xn, g_ref[...]).astype(outs[-1].dtype)


def _down_proj(a, w, layer, res, gain, last, tm=256):
    t, k = a.shape
    d = w.shape[2]
    tm = _pick(t, tm)
    row = pl.BlockSpec((tm, d), lambda i: (i, 0))
    if last:
        out_specs = row
        out_shape = jax.ShapeDtypeStruct((t, d), F32)
    else:
        out_specs = [row, row]
        out_shape = [jax.ShapeDtypeStruct((t, d), F32), jax.ShapeDtypeStruct((t, d), BF16)]
    return pl.pallas_call(
        functools.partial(_down_kernel, not last),
        grid=(t // tm,),
        in_specs=[pl.BlockSpec((tm, k), lambda i: (i, 0)),
                  pl.BlockSpec((None, k, d), lambda i: (layer, 0, 0), pipeline_mode=pl.Buffered(1)),
                  row,
                  pl.BlockSpec((1, d), lambda i: (0, 0))],
        out_specs=out_specs,
        out_shape=out_shape,
        compiler_params=_params("parallel"),
        name="down_proj",
    )(a, w, res, gain.reshape(1, d))


def _conv_kernel(rt, a_ref, wb_in, wc_in, wh_in, cw_ref, gn_ref, o_ref, wb_ref):
    length, cw = o_ref.shape

    @pl.when(pl.program_id(1) == 0)
    def _():
        for gi, w_ref in enumerate((wb_in, wc_in, wh_in)):
            wb_ref[:, gi * cw:(gi + 1) * cw] = w_ref[...].astype(BF16)

    w = cw_ref[...]
    gd = LANES
    def project(r):
        return jnp.dot(a_ref[r * rt:(r + 1) * rt, :], wb_ref[...], preferred_element_type=F32)

    tail = jnp.zeros((SUBLANES, cw), F32)
    n_groups = length // rt
    nxt = project(0)
    for r in range(n_groups):
        rows = slice(r * rt, (r + 1) * rt)
        pr = nxt
        if r + 1 < n_groups:
            nxt = project(r + 1)
        u = pr[:, cw:2 * cw] * pr[:, 2 * cw:]
        full = jnp.concatenate([tail, u], axis=0)
        u1 = pltpu.roll(full, 1, axis=0)[SUBLANES:]
        u2 = pltpu.roll(full, 2, axis=0)[SUBLANES:]
        tail = u[rt - SUBLANES:]
        y = pr[:, :cw] * (u2 * w[0:1] + u1 * w[1:2] + u * w[2:3])
        for gi in range(cw // gd):
            sl = slice(gi * gd, (gi + 1) * gd)
            o_ref[rows, sl] = _rms(y[:, sl], gn_ref[:, sl]).astype(o_ref.dtype)


def _conv_mixer(h, w_in, conv_w, gn, layer, bsz, length, cw=512, rt=512):
    t, d = h.shape
    dc = conv_w.shape[1]
    assert dc // CONV_GROUPS == LANES
    cw = _pick(dc, cw)
    rt = _pick(length, rt)
    nj = dc // cw
    return pl.pallas_call(
        functools.partial(_conv_kernel, rt),
        grid=(nj, bsz),
        in_specs=[pl.BlockSpec((length, d), lambda j, b: (b, 0)),
                  _wtile(d, cw, layer, lambda j: j),
                  _wtile(d, cw, layer, lambda j: nj + j),
                  _wtile(d, cw, layer, lambda j: 2 * nj + j),
                  pl.BlockSpec((conv_w.shape[0], cw), lambda j, b: (0, j)),
                  pl.BlockSpec((1, cw), lambda j, b: (0, j))],
        out_specs=pl.BlockSpec((length, cw), lambda j, b: (b, j)),
        out_shape=jax.ShapeDtypeStruct((t, dc), BF16),
        scratch_shapes=[pltpu.VMEM((d, 3 * cw), BF16)],
        compiler_params=_params("arbitrary", "arbitrary"),
        name="conv_mixer",
    )(h, w_in, w_in, w_in, conv_w, gn.reshape(1, dc))


def _level_table(c):
    t = jnp.arange(c, dtype=jnp.int32)[:, None]
    s = jnp.arange(c, dtype=jnp.int32)[None, :]
    x = t ^ s
    lvl = jnp.zeros((c, c), jnp.int32)
    m = 1
    while m < c:
        lvl = jnp.where(x >= m, m, lvl)
        m *= 2
    return jnp.where(t > s, lvl, 0)


def _tiles(x):
    return [x[j * SUBLANES:(j + 1) * SUBLANES, :] for j in range(x.shape[0] // SUBLANES)]


def _bcast_row(x, r):
    return jnp.broadcast_to(x[r:r + 1, :], x.shape)


def _hgrn_front(q, z, v, g, one_m_lb, lvl):
    c, dk = q.shape
    nv = c // SUBLANES
    sub = lax.broadcasted_iota(jnp.int32, (SUBLANES, dk), 0)

    qf = _silu(q)
    k = one_m_lb * (1.0 / (1.0 + jnp.exp2(z * LOG2E)))
    fc = jnp.maximum(1.0 - k, MIN_FORGET)
    lf = jnp.log2(fc)

    qf_t, k_t, fc_t = _tiles(qf), _tiles(k), _tiles(fc)

    loc_t = []
    for x in _tiles(lf):
        for s in (1, 2, 4):
            x = x + jnp.where(sub >= s, pltpu.roll(x, s, axis=0), 0.0)
        loc_t.append(x)
    b_t = [loc_t[0]]
    for j in range(1, nv):
        b_t.append(loc_t[j] + _bcast_row(b_t[j - 1], SUBLANES - 1))
    b = jnp.concatenate(b_t, axis=0)
    b_last = b_t[nv - 1][SUBLANES - 1:SUBLANES, :]

    def level_operand(m):
        p_t = []
        if m == 1:
            odd = (sub & 1) == 1
            for j in range(nv):
                p_t.append(jnp.where(odd, qf_t[j] * fc_t[j], k_t[j]))
        elif m < SUBLANES:
            is_q = (sub & m) != 0
            for j in range(nv):
                x = loc_t[j]
                if m == 2:
                    ref = jnp.where(sub < 4, _bcast_row(x, 1), _bcast_row(x, 5))
                else:
                    ref = _bcast_row(x, 3)
                em = jnp.exp2(-jnp.abs(x - ref))
                p_t.append(jnp.where(is_q, qf_t[j], k_t[j]) * em)
        else:
            mv = m // SUBLANES
            for j0 in range(0, nv, 2 * mv):
                ref = _bcast_row(b_t[j0 + mv - 1], SUBLANES - 1)
                for j in range(j0, j0 + mv):
                    p_t.append(k_t[j] * jnp.exp2(ref - b_t[j]))
                for j in range(j0 + mv, j0 + 2 * mv):
                    p_t.append(qf_t[j] * jnp.exp2(b_t[j] - ref))
        return jnp.concatenate(p_t, axis=0).astype(BF16)

    zero = jnp.zeros((c, dk), BF16)
    levels = [1 << i for i in range(c.bit_length() - 1)]
    scores = jnp.zeros((c, c), F32)
    for i in range(0, len(levels), 2):
        ms = levels[i:i + 2]
        ps = [level_operand(m) for m in ms]
        if len(ps) == 2:
            lhs = jnp.concatenate(ps, axis=1)
            rhs = jnp.concatenate([jnp.concatenate([ps[0], zero], axis=1),
                                   jnp.concatenate([zero, ps[1]], axis=1)], axis=0)
            sp = _dot_t(lhs, rhs)
        else:
            sp = _dot_t(ps[0], ps[0])
        for j, m in enumerate(ms):
            scores = jnp.where(lvl == m, sp[:, j * c:(j + 1) * c], scores)

    diag = jnp.sum(qf * k, axis=-1, keepdims=True)
    qe = (qf * jnp.exp2(b)).astype(BF16)
    khat_t = (k * jnp.exp2(b_last - b)).astype(BF16).T
    lhs = jnp.concatenate([jnp.concatenate([qe, scores.astype(BF16)], axis=1),
                           jnp.concatenate([jnp.zeros((dk, dk), BF16), khat_t], axis=1)], axis=0)
    decay = jnp.transpose(jnp.broadcast_to(jnp.exp2(b_last), (dk, dk)))
    return lhs, v.astype(BF16), diag * v, decay, _silu(g)


def _hgrn_back(front, gn, st):
    lhs, v_bf, diag_v, decay, gate = front
    c = v_bf.shape[0]
    rhs = jnp.concatenate([st.astype(BF16), v_bf], axis=0)
    res = jnp.dot(lhs, rhs, preferred_element_type=F32)
    out = _rms(res[:c] + diag_v, gn) * gate
    return out, st * decay + res[c:]


def _hgrn_kernel(layer, c, rg, nh, a_ref, wq_in, wz_in, wv_in, wg_in, lbl_ref, gn_ref, lvl_ref,
                 o_ref, wb_ref):
    length, wd = o_ref.shape
    dk = wd // nh

    @pl.when(pl.program_id(1) == 0)
    def _():
        for gi, w_ref in enumerate((wq_in, wz_in, wv_in, wg_in)):
            wb_ref[:, gi * wd:(gi + 1) * wd] = w_ref[...].astype(BF16)

    lg = lbl_ref[...]
    ex = jnp.exp(lg - jnp.max(lg, axis=0, keepdims=True))
    p = ex / jnp.sum(ex, axis=0, keepdims=True)
    cum = jnp.sum(p[:layer + 1], axis=0, keepdims=True)
    lb = jnp.clip(cum - p[0:1], 0.0, 1.0 - 1e-4)
    one_m_lb = 1.0 - lb
    gn = gn_ref[...]
    lvl = lvl_ref[...]

    if length >= 4 * rg:
        sizes = [rg // 2] + [rg] * (length // rg - 1) + [rg // 2]
    else:
        sizes = [rg] * (length // rg)
    starts = [sum(sizes[:r]) for r in range(len(sizes))]

    def project(r, half):
        cs = slice(half * 2 * wd, (half + 1) * 2 * wd)
        return jnp.dot(a_ref[starts[r]:starts[r] + sizes[r], :], wb_ref[:, cs],
                       preferred_element_type=F32)

    def finish(ci, fronts):
        for e in range(nh):
            ls = slice(e * dk, (e + 1) * dk)
            out, sts[e] = _hgrn_back(fronts[e], gn[:, ls], sts[e])
            o_ref[ci * c:(ci + 1) * c, ls] = out.astype(o_ref.dtype)

    sts = [jnp.zeros((dk, dk), F32) for _ in range(nh)]
    n_groups = len(sizes)
    nxt = [project(0, 0), project(0, 1)]
    pending = []
    for r in range(n_groups):
        pr = nxt
        nxt = []
        for i in range(sizes[r] // c):
            rows = slice(i * c, (i + 1) * c)
            fronts = []
            for e in range(nh):
                ls = slice(e * dk, (e + 1) * dk)
                ls2 = slice(wd + e * dk, wd + (e + 1) * dk)
                fronts.append(_hgrn_front(pr[0][rows, ls], pr[0][rows, ls2], pr[1][rows, ls],
                                          pr[1][rows, ls2], one_m_lb[:, ls], lvl))
            pending.append((starts[r] // c + i, fronts))
            if len(pending) > HGRN_SKEW:
                finish(*pending.pop(0))
            if r + 1 < n_groups and i < 2:
                nxt.append(project(r + 1, i))
    for item in pending:
        finish(*item)


def _hgrn_mixer(h, w_in, lb_logits, gn, layer, bsz, length, col0):
    t, d = h.shape
    depth, dh = lb_logits.shape
    dk = dh // HGRN_HEADS
    nh = HGRN_HEADS_PER_STEP
    wd = nh * dk
    c = min(HGRN_CHUNK, length)
    assert HGRN_HEADS % nh == 0 and col0 % nh == 0

    def wcol(gi):
        return _wtile(d, wd, layer, lambda hp: (col0 + gi * HGRN_HEADS) // nh + hp)

    return pl.pallas_call(
        functools.partial(_hgrn_kernel, layer, c, min(HGRN_ROW_GROUP, length), nh),
        grid=(HGRN_HEADS // nh, bsz),
        in_specs=[pl.BlockSpec((length, d), lambda hp, b: (b, 0)),
                  wcol(0), wcol(1), wcol(2), wcol(3),
                  pl.BlockSpec((depth, wd), lambda hp, b: (0, hp)),
                  pl.BlockSpec((1, wd), lambda hp, b: (0, hp)),
                  pl.BlockSpec((c, c), lambda hp, b: (0, 0))],
        out_specs=pl.BlockSpec((length, wd), lambda hp, b: (b, hp)),
        out_shape=jax.ShapeDtypeStruct((t, dh), BF16),
        scratch_shapes=[pltpu.VMEM((d, 4 * wd), BF16)],
        compiler_params=_params("arbitrary", "arbitrary"),
        name="hgrn_mixer",
    )(h, w_in, w_in, w_in, w_in, lb_logits, gn.reshape(1, dh), _level_table(c))


def kernel(x, norm_mix, w_in, conv_w, gn_conv, lb_logits, gn_hgrn, w_out, norm_ffn,
           w_gate, w_up, w_down, norm_final):
    bsz, length, d = x.shape
    depth = w_in.shape[0]
    dc = conv_w.shape[-1]
    dh = lb_logits.shape[-1]
    t = bsz * length
    hgrn_col0 = (3 * dc) // (dh // HGRN_HEADS)

    xr = x.reshape(t, d)
    w_out_bf = w_out.astype(BF16)
    w_down_bf = w_down.astype(BF16)
    h = _norm(xr, norm_mix[0], BF16)
    for l in range(depth):
        conv_out = _conv_mixer(h, w_in, conv_w[l], gn_conv[l], l, bsz, length)
        hgrn_out = _hgrn_mixer(h, w_in, lb_logits, gn_hgrn[l], l, bsz, length, hgrn_col0)
        xr, h = _out_proj(conv_out, hgrn_out, w_out_bf, l, xr, norm_ffn[l])
        act = _glu(h, w_gate, w_up, l)
        last = l == depth - 1
        if last:
            y = _down_proj(act, w_down_bf, l, xr, norm_final, True)
        else:
            xr, h = _down_proj(act, w_down_bf, l, xr, norm_mix[l + 1], False)
    return y.reshape(bsz, length, d)
```

```python
import functools

import jax
import jax.numpy as jnp
from jax import lax
from jax.experimental import pallas as pl
from jax.experimental.pallas import tpu as pltpu

EPS = 1e-6
MIN_FORGET = 1e-30
LOG2E = 1.4426950408889634
CONV_GROUPS = 8
HGRN_HEADS = 8
LANES = 128
SUBLANES = 8
BF16_ROWS = 16
VMEM_LIMIT = 56 * 1024 * 1024
HGRN_CHUNK = 128
HGRN_ROW_GROUP = 512
HGRN_HEADS_PER_STEP = 2
HGRN_SKEW = 1

F32 = jnp.float32
BF16 = jnp.bfloat16


def _params(*sem):
    return pltpu.CompilerParams(dimension_semantics=sem, vmem_limit_bytes=VMEM_LIMIT)


def _pick(n, pref):
    t = min(pref, n)
    while n % t:
        t -= LANES
    return t


def _rms(xf, gain):
    ms = jnp.mean(xf * xf, axis=-1, keepdims=True)
    return xf * lax.rsqrt(ms + EPS) * gain


def _silu(x):
    return x * (1.0 / (1.0 + jnp.exp2(x * (-LOG2E))))


def _dot_t(a, b):
    return lax.dot_general(a, b, (((1,), (1,)), ((), ())), preferred_element_type=F32)


def _wtile(k, tn, layer, col, buffers=1):
    return pl.BlockSpec((None, k, tn), lambda j, i: (layer, 0, col(j)),
                        pipeline_mode=pl.Buffered(buffers))


def _norm_kernel(x_ref, g_ref, o_ref):
    o_ref[...] = _rms(x_ref[...], g_ref[...]).astype(o_ref.dtype)


def _norm(x, gain, out_dtype, tm=512):
    t, d = x.shape
    tm = _pick(t, tm)
    return pl.pallas_call(
        _norm_kernel,
        grid=(t // tm,),
        in_specs=[pl.BlockSpec((tm, d), lambda i: (i, 0)),
                  pl.BlockSpec((1, d), lambda i: (0, 0))],
        out_specs=pl.BlockSpec((tm, d), lambda i: (i, 0)),
        out_shape=jax.ShapeDtypeStruct((t, d), out_dtype),
        compiler_params=_params("parallel"),
        name="rms_norm",
    )(x, gain.reshape(1, d))


def _glu_kernel(a_ref, wg_ref, wu_ref, side_ref, o_ref, side_o_ref, wgb_ref, wub_ref):
    @pl.when(pl.program_id(1) == 0)
    def _():
        wgb_ref[...] = wg_ref[...].astype(BF16)
        wub_ref[...] = wu_ref[...].astype(BF16)

    side_o_ref[...] = side_ref[...].astype(BF16)
    a = a_ref[...]
    g = jnp.dot(a, wgb_ref[...], preferred_element_type=F32)
    u = jnp.dot(a, wub_ref[...], preferred_element_type=F32)
    o_ref[...] = (_silu(g) * u).astype(o_ref.dtype)


def _side_cast_specs(w, layer, steps, inner):
    rows, cols = w.shape[1:]
    rb = rows // steps
    assert rows % steps == 0 and rb % BF16_ROWS == 0
    return (pl.BlockSpec((None, rb, cols), lambda j, i: (layer, j * inner + i, 0)),
            pl.BlockSpec((rb, cols), lambda j, i: (j * inner + i, 0)),
            jax.ShapeDtypeStruct((rows, cols), BF16))


def _glu(a, wg, wu, w_side, layer, tm=1024, tn=512):
    t, k = a.shape
    n = wg.shape[2]
    tm, tn = _pick(t, tm), _pick(n, tn)
    side_in, side_out, side_shape = _side_cast_specs(w_side, layer, (n // tn) * (t // tm), t // tm)
    return pl.pallas_call(
        _glu_kernel,
        grid=(n // tn, t // tm),
        in_specs=[pl.BlockSpec((tm, k), lambda j, i: (i, 0)),
                  _wtile(k, tn, layer, lambda j: j, buffers=2),
                  _wtile(k, tn, layer, lambda j: j, buffers=2),
                  side_in],
        out_specs=[pl.BlockSpec((tm, tn), lambda j, i: (i, j)), side_out],
        out_shape=[jax.ShapeDtypeStruct((t, n), BF16), side_shape],
        scratch_shapes=[pltpu.VMEM((k, tn), BF16), pltpu.VMEM((k, tn), BF16)],
        compiler_params=_params("arbitrary", "arbitrary"),
        name="ffn_glu",
    )(a, wg, wu, w_side)


def _out_kernel(a1_ref, a2_ref, w1_ref, w2_ref, r_ref, g_ref, x_ref, h_ref):
    acc = jnp.dot(a1_ref[...], w1_ref[...], preferred_element_type=F32)
    acc = acc + jnp.dot(a2_ref[...], w2_ref[...], preferred_element_type=F32)
    xn = r_ref[...] + acc
    x_ref[...] = xn
    h_ref[...] = _rms(xn, g_ref[...]).astype(h_ref.dtype)


def _out_proj(a1, a2, w, res, gain, tm=512):
    t, k1 = a1.shape
    k2 = a2.shape[1]
    d = w.shape[1]
    assert k1 == k2
    tm = _pick(t, tm)
    return pl.pallas_call(
        _out_kernel,
        grid=(t // tm,),
        in_specs=[pl.BlockSpec((tm, k1), lambda i: (i, 0)),
                  pl.BlockSpec((tm, k2), lambda i: (i, 0)),
                  pl.BlockSpec((k1, d), lambda i: (0, 0)),
                  pl.BlockSpec((k2, d), lambda i: (1, 0)),
                  pl.BlockSpec((tm, d), lambda i: (i, 0)),
                  pl.BlockSpec((1, d), lambda i: (0, 0))],
        out_specs=[pl.BlockSpec((tm, d), lambda i: (i, 0)),
                   pl.BlockSpec((tm, d), lambda i: (i, 0))],
        out_shape=[jax.ShapeDtypeStruct((t, d), F32),
                   jax.ShapeDtypeStruct((t, d), BF16)],
        compiler_params=_params("parallel"),
        name="out_proj",
    )(a1, a2, w, w, res, gain.reshape(1, d))


def _down_kernel(emit_x, a_ref, w_ref, r_ref, g_ref, *outs):
    xn = r_ref[...] + jnp.dot(a_ref[...], w_ref[...], preferred_element_type=F32)
    if emit_x:
        outs[0][...] = xn
    outs[-1][...] = _rms(xn, g_ref[...]).astype(outs[-1].dtype)


def _down_proj(a, w, res, gain, last, tm=256):
    t, k = a.shape
    d = w.shape[1]
    tm = _pick(t, tm)
    row = pl.BlockSpec((tm, d), lambda i: (i, 0))
    if last:
        out_specs = row
        out_shape = jax.ShapeDtypeStruct((t, d), F32)
    else:
        out_specs = [row, row]
        out_shape = [jax.ShapeDtypeStruct((t, d), F32), jax.ShapeDtypeStruct((t, d), BF16)]
    return pl.pallas_call(
        functools.partial(_down_kernel, not last),
        grid=(t // tm,),
        in_specs=[pl.BlockSpec((tm, k), lambda i: (i, 0)),
                  pl.BlockSpec((k, d), lambda i: (0, 0), pipeline_mode=pl.Buffered(1)),
                  row,
                  pl.BlockSpec((1, d), lambda i: (0, 0))],
        out_specs=out_specs,
        out_shape=out_shape,
        compiler_params=_params("parallel"),
        name="down_proj",
    )(a, w, res, gain.reshape(1, d))


def _conv_kernel(rt, a_ref, wb_in, wc_in, wh_in, cw_ref, gn_ref, side_ref, o_ref, side_o_ref, wb_ref):
    length, cw = o_ref.shape
    side_o_ref[...] = side_ref[...].astype(BF16)

    @pl.when(pl.program_id(1) == 0)
    def _():
        for gi, w_ref in enumerate((wb_in, wc_in, wh_in)):
            wb_ref[:, gi * cw:(gi + 1) * cw] = w_ref[...].astype(BF16)

    w = cw_ref[...]
    gd = LANES
    def project(r):
        return jnp.dot(a_ref[r * rt:(r + 1) * rt, :], wb_ref[...], preferred_element_type=F32)

    tail = jnp.zeros((SUBLANES, cw), F32)
    n_groups = length // rt
    nxt = project(0)
    for r in range(n_groups):
        rows = slice(r * rt, (r + 1) * rt)
        pr = nxt
        if r + 1 < n_groups:
            nxt = project(r + 1)
        u = pr[:, cw:2 * cw] * pr[:, 2 * cw:]
        full = jnp.concatenate([tail, u], axis=0)
        u1 = pltpu.roll(full, 1, axis=0)[SUBLANES:]
        u2 = pltpu.roll(full, 2, axis=0)[SUBLANES:]
        tail = u[rt - SUBLANES:]
        y = pr[:, :cw] * (u2 * w[0:1] + u1 * w[1:2] + u * w[2:3])
        for gi in range(cw // gd):
            sl = slice(gi * gd, (gi + 1) * gd)
            o_ref[rows, sl] = _rms(y[:, sl], gn_ref[:, sl]).astype(o_ref.dtype)


def _conv_mixer(h, w_in, conv_w, gn, w_side, layer, bsz, length, cw=512, rt=512):
    t, d = h.shape
    dc = conv_w.shape[1]
    assert dc // CONV_GROUPS == LANES
    cw = _pick(dc, cw)
    rt = _pick(length, rt)
    nj = dc // cw
    side_in, side_out, side_shape = _side_cast_specs(w_side, layer, nj * bsz, bsz)
    return pl.pallas_call(
        functools.partial(_conv_kernel, rt),
        grid=(nj, bsz),
        in_specs=[pl.BlockSpec((length, d), lambda j, b: (b, 0)),
                  _wtile(d, cw, layer, lambda j: j),
                  _wtile(d, cw, layer, lambda j: nj + j),
                  _wtile(d, cw, layer, lambda j: 2 * nj + j),
                  pl.BlockSpec((conv_w.shape[0], cw), lambda j, b: (0, j)),
                  pl.BlockSpec((1, cw), lambda j, b: (0, j)),
                  side_in],
        out_specs=[pl.BlockSpec((length, cw), lambda j, b: (b, j)), side_out],
        out_shape=[jax.ShapeDtypeStruct((t, dc), BF16), side_shape],
        scratch_shapes=[pltpu.VMEM((d, 3 * cw), BF16)],
        compiler_params=_params("arbitrary", "arbitrary"),
        name="conv_mixer",
    )(h, w_in, w_in, w_in, conv_w, gn.reshape(1, dc), w_side)


def _level_table(c):
    t = jnp.arange(c, dtype=jnp.int32)[:, None]
    s = jnp.arange(c, dtype=jnp.int32)[None, :]
    x = t ^ s
    lvl = jnp.zeros((c, c), jnp.int32)
    m = 1
    while m < c:
        lvl = jnp.where(x >= m, m, lvl)
        m *= 2
    return jnp.where(t > s, lvl, 0)


def _tiles(x):
    return [x[j * SUBLANES:(j + 1) * SUBLANES, :] for j in range(x.shape[0] // SUBLANES)]


def _bcast_row(x, r):
    return jnp.broadcast_to(x[r:r + 1, :], x.shape)


def _hgrn_front(q, z, v, g, one_m_lb, lvl):
    c, dk = q.shape
    nv = c // SUBLANES
    sub = lax.broadcasted_iota(jnp.int32, (SUBLANES, dk), 0)

    qf = _silu(q)
    k = one_m_lb * (1.0 / (1.0 + jnp.exp2(z * LOG2E)))
    fc = jnp.maximum(1.0 - k, MIN_FORGET)
    lf = jnp.log2(fc)

    qf_t, k_t, fc_t = _tiles(qf), _tiles(k), _tiles(fc)

    loc_t = []
    for x in _tiles(lf):
        for s in (1, 2, 4):
            x = x + jnp.where(sub >= s, pltpu.roll(x, s, axis=0), 0.0)
        loc_t.append(x)
    b_t = [loc_t[0]]
    for j in range(1, nv):
        b_t.append(loc_t[j] + _bcast_row(b_t[j - 1], SUBLANES - 1))
    b = jnp.concatenate(b_t, axis=0)
    b_last = b_t[nv - 1][SUBLANES - 1:SUBLANES, :]

    def level_operand(m):
        p_t = []
        if m == 1:
            odd = (sub & 1) == 1
            for j in range(nv):
                p_t.append(jnp.where(odd, qf_t[j] * fc_t[j], k_t[j]))
        elif m < SUBLANES:
            is_q = (sub & m) != 0
            for j in range(nv):
                x = loc_t[j]
                if m == 2:
                    ref = jnp.where(sub < 4, _bcast_row(x, 1), _bcast_row(x, 5))
                else:
                    ref = _bcast_row(x, 3)
                em = jnp.exp2(-jnp.abs(x - ref))
                p_t.append(jnp.where(is_q, qf_t[j], k_t[j]) * em)
        else:
            mv = m // SUBLANES
            for j0 in range(0, nv, 2 * mv):
                ref = _bcast_row(b_t[j0 + mv - 1], SUBLANES - 1)
                for j in range(j0, j0 + mv):
                    p_t.append(k_t[j] * jnp.exp2(ref - b_t[j]))
                for j in range(j0 + mv, j0 + 2 * mv):
                    p_t.append(qf_t[j] * jnp.exp2(b_t[j] - ref))
        return jnp.concatenate(p_t, axis=0).astype(BF16)

    zero = jnp.zeros((c, dk), BF16)
    levels = [1 << i for i in range(c.bit_length() - 1)]
    scores = jnp.zeros((c, c), F32)
    for i in range(0, len(levels), 2):
        ms = levels[i:i + 2]
        ps = [level_operand(m) for m in ms]
        if len(ps) == 2:
            lhs = jnp.concatenate(ps, axis=1)
            rhs = jnp.concatenate([jnp.concatenate([ps[0], zero], axis=1),
                                   jnp.concatenate([zero, ps[1]], axis=1)], axis=0)
            sp = _dot_t(lhs, rhs)
        else:
            sp = _dot_t(ps[0], ps[0])
        for j, m in enumerate(ms):
            scores = jnp.where(lvl == m, sp[:, j * c:(j + 1) * c], scores)

    diag = jnp.sum(qf * k, axis=-1, keepdims=True)
    qe = (qf * jnp.exp2(b)).astype(BF16)
    khat_t = (k * jnp.exp2(b_last - b)).astype(BF16).T
    lhs = jnp.concatenate([jnp.concatenate([qe, scores.astype(BF16)], axis=1),
                           jnp.concatenate([jnp.zeros((dk, dk), BF16), khat_t], axis=1)], axis=0)
    decay = jnp.transpose(jnp.broadcast_to(jnp.exp2(b_last), (dk, dk)))
    return lhs, v.astype(BF16), diag * v, decay, _silu(g)


def _hgrn_back(front, gn, st):
    lhs, v_bf, diag_v, decay, gate = front
    c = v_bf.shape[0]
    rhs = jnp.concatenate([st.astype(BF16), v_bf], axis=0)
    res = jnp.dot(lhs, rhs, preferred_element_type=F32)
    out = _rms(res[:c] + diag_v, gn) * gate
    return out, st * decay + res[c:]


def _hgrn_kernel(layer, c, rg, nh, a_ref, wq_in, wz_in, wv_in, wg_in, lbl_ref, gn_ref, lvl_ref,
                 o_ref, wb_ref):
    length, wd = o_ref.shape
    dk = wd // nh

    @pl.when(pl.program_id(1) == 0)
    def _():
        for gi, w_ref in enumerate((wq_in, wz_in, wv_in, wg_in)):
            wb_ref[:, gi * wd:(gi + 1) * wd] = w_ref[...].astype(BF16)

    lg = lbl_ref[...]
    ex = jnp.exp(lg - jnp.max(lg, axis=0, keepdims=True))
    p = ex / jnp.sum(ex, axis=0, keepdims=True)
    cum = jnp.sum(p[:layer + 1], axis=0, keepdims=True)
    lb = jnp.clip(cum - p[0:1], 0.0, 1.0 - 1e-4)
    one_m_lb = 1.0 - lb
    gn = gn_ref[...]
    lvl = lvl_ref[...]

    if length >= 4 * rg:
        sizes = [rg // 2] + [rg] * (length // rg - 1) + [rg // 2]
    else:
        sizes = [rg] * (length // rg)
    starts = [sum(sizes[:r]) for r in range(len(sizes))]

    def project(r, half):
        cs = slice(half * 2 * wd, (half + 1) * 2 * wd)
        return jnp.dot(a_ref[starts[r]:starts[r] + sizes[r], :], wb_ref[:, cs],
                       preferred_element_type=F32)

    def finish(ci, fronts):
        for e in range(nh):
            ls = slice(e * dk, (e + 1) * dk)
            out, sts[e] = _hgrn_back(fronts[e], gn[:, ls], sts[e])
            o_ref[ci * c:(ci + 1) * c, ls] = out.astype(o_ref.dtype)

    sts = [jnp.zeros((dk, dk), F32) for _ in range(nh)]
    n_groups = len(sizes)
    nxt = [project(0, 0), project(0, 1)]
    pending = []
    for r in range(n_groups):
        pr = nxt
        nxt = []
        for i in range(sizes[r] // c):
            rows = slice(i * c, (i + 1) * c)
            fronts = []
            for e in range(nh):
                ls = slice(e * dk, (e + 1) * dk)
                ls2 = slice(wd + e * dk, wd + (e + 1) * dk)
                fronts.append(_hgrn_front(pr[0][rows, ls], pr[0][rows, ls2], pr[1][rows, ls],
                                          pr[1][rows, ls2], one_m_lb[:, ls], lvl))
            pending.append((starts[r] // c + i, fronts))
            if len(pending) > HGRN_SKEW:
                finish(*pending.pop(0))
            if r + 1 < n_groups and i < 2:
                nxt.append(project(r + 1, i))
    for item in pending:
        finish(*item)


def _hgrn_mixer(h, w_in, lb_logits, gn, layer, bsz, length, col0):
    t, d = h.shape
    depth, dh = lb_logits.shape
    dk = dh // HGRN_HEADS
    nh = HGRN_HEADS_PER_STEP
    wd = nh * dk
    c = min(HGRN_CHUNK, length)
    assert HGRN_HEADS % nh == 0 and col0 % nh == 0

    def wcol(gi):
        return _wtile(d, wd, layer, lambda hp: (col0 + gi * HGRN_HEADS) // nh + hp)

    return pl.pallas_call(
        functools.partial(_hgrn_kernel, layer, c, min(HGRN_ROW_GROUP, length), nh),
        grid=(HGRN_HEADS // nh, bsz),
        in_specs=[pl.BlockSpec((length, d), lambda hp, b: (b, 0)),
                  wcol(0), wcol(1), wcol(2), wcol(3),
                  pl.BlockSpec((depth, wd), lambda hp, b: (0, hp)),
                  pl.BlockSpec((1, wd), lambda hp, b: (0, hp)),
                  pl.BlockSpec((c, c), lambda hp, b: (0, 0))],
        out_specs=pl.BlockSpec((length, wd), lambda hp, b: (b, hp)),
        out_shape=jax.ShapeDtypeStruct((t, dh), BF16),
        scratch_shapes=[pltpu.VMEM((d, 4 * wd), BF16)],
        compiler_params=_params("arbitrary", "arbitrary"),
        name="hgrn_mixer",
    )(h, w_in, w_in, w_in, w_in, lb_logits, gn.reshape(1, dh), _level_table(c))


def kernel(x, norm_mix, w_in, conv_w, gn_conv, lb_logits, gn_hgrn, w_out, norm_ffn,
           w_gate, w_up, w_down, norm_final):
    bsz, length, d = x.shape
    depth = w_in.shape[0]
    dc = conv_w.shape[-1]
    dh = lb_logits.shape[-1]
    t = bsz * length
    hgrn_col0 = (3 * dc) // (dh // HGRN_HEADS)

    xr = x.reshape(t, d)
    h = _norm(xr, norm_mix[0], BF16)
    for l in range(depth):
        conv_out, w_out_bf = _conv_mixer(h, w_in, conv_w[l], gn_conv[l], w_out, l, bsz, length)
        hgrn_out = _hgrn_mixer(h, w_in, lb_logits, gn_hgrn[l], l, bsz, length, hgrn_col0)
        xr, h = _out_proj(conv_out, hgrn_out, w_out_bf, xr, norm_ffn[l])
        act, w_down_bf = _glu(h, w_gate, w_up, w_down, l)
        last = l == depth - 1
        if last:
            y = _down_proj(act, w_down_bf, xr, norm_final, True)
        else:
            xr, h = _down_proj(act, w_down_bf, xr, norm_mix[l + 1], False)
    return y.reshape(bsz, length, d)
```

```python
import functools

import jax
import jax.numpy as jnp
from jax import lax
from jax.experimental import pallas as pl
from jax.experimental.pallas import tpu as pltpu

EPS = 1e-6
MIN_FORGET = 1e-30
LOG2E = 1.4426950408889634
CONV_GROUPS = 8
HGRN_HEADS = 8
LANES = 128
SUBLANES = 8
BF16_ROWS = 16
VMEM_LIMIT = 56 * 1024 * 1024
HGRN_CHUNK = 128
HGRN_ROW_GROUP = 512
HGRN_HEADS_PER_STEP = 2
HGRN_SKEW = 1

NORM_ROWS = 512
GLU_ROWS, GLU_COLS = 1024, 512
OUT_ROWS = 512
DOWN_ROWS = 256
CONV_COLS, CONV_ROWS = 512, 512

F32 = jnp.float32
BF16 = jnp.bfloat16


def _params(*sem):
    return pltpu.CompilerParams(dimension_semantics=sem, vmem_limit_bytes=VMEM_LIMIT)


def _pick(n, pref):
    t = min(pref, n)
    while n % t:
        t -= LANES
    return t


def _rms(xf, gain):
    ms = jnp.mean(xf * xf, axis=-1, keepdims=True)
    return xf * lax.rsqrt(ms + EPS) * gain


def _silu(x):
    return x * (1.0 / (1.0 + jnp.exp2(x * (-LOG2E))))


def _dot_t(a, b):
    return lax.dot_general(a, b, (((1,), (1,)), ((), ())), preferred_element_type=F32)


def _wtile(k, tn, layer, col, buffers=1):
    return pl.BlockSpec((None, k, tn), lambda j, i: (layer, 0, col(j)),
                        pipeline_mode=pl.Buffered(buffers))


def _norm_kernel(x_ref, g_ref, o_ref):
    o_ref[...] = _rms(x_ref[...], g_ref[...]).astype(o_ref.dtype)


def _norm(x, gain, out_dtype):
    t, d = x.shape
    tm = _pick(t, NORM_ROWS)
    return pl.pallas_call(
        _norm_kernel,
        grid=(t // tm,),
        in_specs=[pl.BlockSpec((tm, d), lambda i: (i, 0)),
                  pl.BlockSpec((1, d), lambda i: (0, 0))],
        out_specs=pl.BlockSpec((tm, d), lambda i: (i, 0)),
        out_shape=jax.ShapeDtypeStruct((t, d), out_dtype),
        compiler_params=_params("parallel"),
        name="rms_norm",
    )(x, gain.reshape(1, d))


def _glu_kernel(a_ref, wg_ref, wu_ref, o_ref, wgb_ref, wub_ref):
    @pl.when(pl.program_id(1) == 0)
    def _():
        wgb_ref[...] = wg_ref[...].astype(BF16)
        wub_ref[...] = wu_ref[...].astype(BF16)

    a = a_ref[...]
    g = jnp.dot(a, wgb_ref[...], preferred_element_type=F32)
    u = jnp.dot(a, wub_ref[...], preferred_element_type=F32)
    o_ref[...] = (_silu(g) * u).astype(o_ref.dtype)


def _side_cast_specs(w, layer, steps, inner):
    rows, cols = w.shape[1:]
    rb = rows // steps
    assert rows % steps == 0 and rb % BF16_ROWS == 0
    return (pl.BlockSpec((None, rb, cols), lambda j, i: (layer, j * inner + i, 0)),
            pl.BlockSpec((rb, cols), lambda j, i: (j * inner + i, 0)),
            jax.ShapeDtypeStruct((rows, cols), BF16))


def _glu(a, wg, wu, layer):
    t, k = a.shape
    n = wg.shape[2]
    tm, tn = _pick(t, GLU_ROWS), _pick(n, GLU_COLS)
    return pl.pallas_call(
        _glu_kernel,
        grid=(n // tn, t // tm),
        in_specs=[pl.BlockSpec((tm, k), lambda j, i: (i, 0)),
                  _wtile(k, tn, layer, lambda j: j, buffers=2),
                  _wtile(k, tn, layer, lambda j: j, buffers=2)],
        out_specs=pl.BlockSpec((tm, tn), lambda j, i: (i, j)),
        out_shape=jax.ShapeDtypeStruct((t, n), BF16),
        scratch_shapes=[pltpu.VMEM((k, tn), BF16), pltpu.VMEM((k, tn), BF16)],
        compiler_params=_params("arbitrary", "arbitrary"),
        name="ffn_glu",
    )(a, wg, wu)


def _out_kernel(a1_ref, a2_ref, w1_ref, w2_ref, r_ref, g_ref, x_ref, h_ref):
    acc = jnp.dot(a1_ref[...], w1_ref[...], preferred_element_type=F32)
    acc = acc + jnp.dot(a2_ref[...], w2_ref[...], preferred_element_type=F32)
    xn = r_ref[...] + acc
    x_ref[...] = xn
    h_ref[...] = _rms(xn, g_ref[...]).astype(h_ref.dtype)


def _out_proj(a1, a2, w, res, gain):
    t, k1 = a1.shape
    k2 = a2.shape[1]
    d = w.shape[1]
    assert k1 == k2
    tm = _pick(t, OUT_ROWS)
    return pl.pallas_call(
        _out_kernel,
        grid=(t // tm,),
        in_specs=[pl.BlockSpec((tm, k1), lambda i: (i, 0)),
                  pl.BlockSpec((tm, k2), lambda i: (i, 0)),
                  pl.BlockSpec((k1, d), lambda i: (0, 0)),
                  pl.BlockSpec((k2, d), lambda i: (1, 0)),
                  pl.BlockSpec((tm, d), lambda i: (i, 0)),
                  pl.BlockSpec((1, d), lambda i: (0, 0))],
        out_specs=[pl.BlockSpec((tm, d), lambda i: (i, 0)),
                   pl.BlockSpec((tm, d), lambda i: (i, 0))],
        out_shape=[jax.ShapeDtypeStruct((t, d), F32),
                   jax.ShapeDtypeStruct((t, d), BF16)],
        compiler_params=_params("parallel"),
        name="out_proj",
    )(a1, a2, w, w, res, gain.reshape(1, d))


def _down_kernel(emit_x, a_ref, w_ref, r_ref, g_ref, *outs):
    xn = r_ref[...] + jnp.dot(a_ref[...], w_ref[...], preferred_element_type=F32)
    if emit_x:
        outs[0][...] = xn
    outs[-1][...] = _rms(xn, g_ref[...]).astype(outs[-1].dtype)


def _down_proj(a, w, res, gain, last):
    t, k = a.shape
    d = w.shape[1]
    tm = _pick(t, DOWN_ROWS)
    row = pl.BlockSpec((tm, d), lambda i: (i, 0))
    if last:
        out_specs = row
        out_shape = jax.ShapeDtypeStruct((t, d), F32)
    else:
        out_specs = [row, row]
        out_shape = [jax.ShapeDtypeStruct((t, d), F32), jax.ShapeDtypeStruct((t, d), BF16)]
    return pl.pallas_call(
        functools.partial(_down_kernel, not last),
        grid=(t // tm,),
        in_specs=[pl.BlockSpec((tm, k), lambda i: (i, 0)),
                  pl.BlockSpec((k, d), lambda i: (0, 0), pipeline_mode=pl.Buffered(1)),
                  row,
                  pl.BlockSpec((1, d), lambda i: (0, 0))],
        out_specs=out_specs,
        out_shape=out_shape,
        compiler_params=_params("parallel"),
        name="down_proj",
    )(a, w, res, gain.reshape(1, d))


def _conv_kernel(rt, a_ref, wb_in, wc_in, wh_in, cw_ref, gn_ref, s1_ref, s2_ref,
                 o_ref, s1_o_ref, s2_o_ref, wb_ref):
    length, cw = o_ref.shape
    s1_o_ref[...] = s1_ref[...].astype(BF16)
    s2_o_ref[...] = s2_ref[...].astype(BF16)

    @pl.when(pl.program_id(1) == 0)
    def _():
        for gi, w_ref in enumerate((wb_in, wc_in, wh_in)):
            wb_ref[:, gi * cw:(gi + 1) * cw] = w_ref[...].astype(BF16)

    w = cw_ref[...]
    gd = LANES
    def project(r):
        return jnp.dot(a_ref[r * rt:(r + 1) * rt, :], wb_ref[...], preferred_element_type=F32)

    tail = jnp.zeros((SUBLANES, cw), F32)
    n_groups = length // rt
    nxt = project(0)
    for r in range(n_groups):
        rows = slice(r * rt, (r + 1) * rt)
        pr = nxt
        if r + 1 < n_groups:
            nxt = project(r + 1)
        u = pr[:, cw:2 * cw] * pr[:, 2 * cw:]
        full = jnp.concatenate([tail, u], axis=0)
        u1 = pltpu.roll(full, 1, axis=0)[SUBLANES:]
        u2 = pltpu.roll(full, 2, axis=0)[SUBLANES:]
        tail = u[rt - SUBLANES:]
        y = pr[:, :cw] * (u2 * w[0:1] + u1 * w[1:2] + u * w[2:3])
        for gi in range(cw // gd):
            sl = slice(gi * gd, (gi + 1) * gd)
            o_ref[rows, sl] = _rms(y[:, sl], gn_ref[:, sl]).astype(o_ref.dtype)


def _conv_mixer(h, w_in, conv_w, gn, w_side1, w_side2, layer, bsz, length):
    t, d = h.shape
    dc = conv_w.shape[1]
    assert dc // CONV_GROUPS == LANES
    cw = _pick(dc, CONV_COLS)
    rt = _pick(length, CONV_ROWS)
    nj = dc // cw
    s1_in, s1_out, s1_shape = _side_cast_specs(w_side1, layer, nj * bsz, bsz)
    s2_in, s2_out, s2_shape = _side_cast_specs(w_side2, layer, nj * bsz, bsz)
    return pl.pallas_call(
        functools.partial(_conv_kernel, rt),
        grid=(nj, bsz),
        in_specs=[pl.BlockSpec((length, d), lambda j, b: (b, 0)),
                  _wtile(d, cw, layer, lambda j: j),
                  _wtile(d, cw, layer, lambda j: nj + j),
                  _wtile(d, cw, layer, lambda j: 2 * nj + j),
                  pl.BlockSpec((conv_w.shape[0], cw), lambda j, b: (0, j)),
                  pl.BlockSpec((1, cw), lambda j, b: (0, j)),
                  s1_in, s2_in],
        out_specs=[pl.BlockSpec((length, cw), lambda j, b: (b, j)), s1_out, s2_out],
        out_shape=[jax.ShapeDtypeStruct((t, dc), BF16), s1_shape, s2_shape],
        scratch_shapes=[pltpu.VMEM((d, 3 * cw), BF16)],
        compiler_params=_params("arbitrary", "arbitrary"),
        name="conv_mixer",
    )(h, w_in, w_in, w_in, conv_w, gn.reshape(1, dc), w_side1, w_side2)


def _level_table(c):
    t = jnp.arange(c, dtype=jnp.int32)[:, None]
    s = jnp.arange(c, dtype=jnp.int32)[None, :]
    x = t ^ s
    lvl = jnp.zeros((c, c), jnp.int32)
    m = 1
    while m < c:
        lvl = jnp.where(x >= m, m, lvl)
        m *= 2
    return jnp.where(t > s, lvl, 0)


def _tiles(x):
    return [x[j * SUBLANES:(j + 1) * SUBLANES, :] for j in range(x.shape[0] // SUBLANES)]


def _bcast_row(x, r):
    return jnp.broadcast_to(x[r:r + 1, :], x.shape)


def _hgrn_front(q, z, v, g, one_m_lb, lvl):
    c, dk = q.shape
    nv = c // SUBLANES
    sub = lax.broadcasted_iota(jnp.int32, (SUBLANES, dk), 0)

    qf = _silu(q)
    k = one_m_lb * (1.0 / (1.0 + jnp.exp2(z * LOG2E)))
    fc = jnp.maximum(1.0 - k, MIN_FORGET)
    lf = jnp.log2(fc)

    qf_t, k_t, fc_t = _tiles(qf), _tiles(k), _tiles(fc)

    loc_t = []
    for x in _tiles(lf):
        for s in (1, 2, 4):
            x = x + jnp.where(sub >= s, pltpu.roll(x, s, axis=0), 0.0)
        loc_t.append(x)
    b_t = [loc_t[0]]
    for j in range(1, nv):
        b_t.append(loc_t[j] + _bcast_row(b_t[j - 1], SUBLANES - 1))
    b = jnp.concatenate(b_t, axis=0)
    b_last = b_t[nv - 1][SUBLANES - 1:SUBLANES, :]

    def level_operand(m):
        p_t = []
        if m == 1:
            odd = (sub & 1) == 1
            for j in range(nv):
                p_t.append(jnp.where(odd, qf_t[j] * fc_t[j], k_t[j]))
        elif m < SUBLANES:
            is_q = (sub & m) != 0
            for j in range(nv):
                x = loc_t[j]
                if m == 2:
                    ref = jnp.where(sub < 4, _bcast_row(x, 1), _bcast_row(x, 5))
                else:
                    ref = _bcast_row(x, 3)
                em = jnp.exp2(-jnp.abs(x - ref))
                p_t.append(jnp.where(is_q, qf_t[j], k_t[j]) * em)
        else:
            mv = m // SUBLANES
            for j0 in range(0, nv, 2 * mv):
                ref = _bcast_row(b_t[j0 + mv - 1], SUBLANES - 1)
                for j in range(j0, j0 + mv):
                    p_t.append(k_t[j] * jnp.exp2(ref - b_t[j]))
                for j in range(j0 + mv, j0 + 2 * mv):
                    p_t.append(qf_t[j] * jnp.exp2(b_t[j] - ref))
        return jnp.concatenate(p_t, axis=0).astype(BF16)

    zero = jnp.zeros((c, dk), BF16)
    levels = [1 << i for i in range(c.bit_length() - 1)]
    scores = jnp.zeros((c, c), F32)
    for i in range(0, len(levels), 2):
        ms = levels[i:i + 2]
        ps = [level_operand(m) for m in ms]
        if len(ps) == 2:
            lhs = jnp.concatenate(ps, axis=1)
            rhs = jnp.concatenate([jnp.concatenate([ps[0], zero], axis=1),
                                   jnp.concatenate([zero, ps[1]], axis=1)], axis=0)
            sp = _dot_t(lhs, rhs)
        else:
            sp = _dot_t(ps[0], ps[0])
        for j, m in enumerate(ms):
            scores = jnp.where(lvl == m, sp[:, j * c:(j + 1) * c], scores)

    diag = jnp.sum(qf * k, axis=-1, keepdims=True)
    qe = (qf * jnp.exp2(b)).astype(BF16)
    khat_t = (k * jnp.exp2(b_last - b)).astype(BF16).T
    lhs = jnp.concatenate([jnp.concatenate([qe, scores.astype(BF16)], axis=1),
                           jnp.concatenate([jnp.zeros((dk, dk), BF16), khat_t], axis=1)], axis=0)
    decay = jnp.transpose(jnp.broadcast_to(jnp.exp2(b_last), (dk, dk)))
    return lhs, v.astype(BF16), diag * v, decay, _silu(g)


def _hgrn_back(front, gn, st):
    lhs, v_bf, diag_v, decay, gate = front
    c = v_bf.shape[0]
    rhs = jnp.concatenate([st.astype(BF16), v_bf], axis=0)
    res = jnp.dot(lhs, rhs, preferred_element_type=F32)
    out = _rms(res[:c] + diag_v, gn) * gate
    return out, st * decay + res[c:]


def _hgrn_kernel(layer, c, rg, nh, a_ref, wq_in, wz_in, wv_in, wg_in, lbl_ref, gn_ref, lvl_ref,
                 o_ref, wb_ref):
    length, wd = o_ref.shape
    dk = wd // nh

    @pl.when(pl.program_id(1) == 0)
    def _():
        for gi, w_ref in enumerate((wq_in, wz_in, wv_in, wg_in)):
            wb_ref[:, gi * wd:(gi + 1) * wd] = w_ref[...].astype(BF16)

    lg = lbl_ref[...]
    ex = jnp.exp(lg - jnp.max(lg, axis=0, keepdims=True))
    p = ex / jnp.sum(ex, axis=0, keepdims=True)
    cum = jnp.sum(p[:layer + 1], axis=0, keepdims=True)
    lb = jnp.clip(cum - p[0:1], 0.0, 1.0 - 1e-4)
    one_m_lb = 1.0 - lb
    gn = gn_ref[...]
    lvl = lvl_ref[...]

    if length >= 4 * rg:
        sizes = [rg // 2] + [rg] * (length // rg - 1) + [rg // 2]
    else:
        sizes = [rg] * (length // rg)
    starts = [sum(sizes[:r]) for r in range(len(sizes))]

    def project(r, half):
        cs = slice(half * 2 * wd, (half + 1) * 2 * wd)
        return jnp.dot(a_ref[starts[r]:starts[r] + sizes[r], :], wb_ref[:, cs],
                       preferred_element_type=F32)

    def finish(ci, e, front):
        ls = slice(e * dk, (e + 1) * dk)
        out, sts[e] = _hgrn_back(front, gn[:, ls], sts[e])
        o_ref[ci * c:(ci + 1) * c, ls] = out.astype(o_ref.dtype)

    sts = [jnp.zeros((dk, dk), F32) for _ in range(nh)]
    n_groups = len(sizes)
    nxt = [project(0, 0), project(0, 1)]
    pending = []
    for r in range(n_groups):
        pr = nxt
        nxt = []
        for i in range(sizes[r] // c):
            rows = slice(i * c, (i + 1) * c)
            for e in range(nh):
                ls = slice(e * dk, (e + 1) * dk)
                ls2 = slice(wd + e * dk, wd + (e + 1) * dk)
                front = _hgrn_front(pr[0][rows, ls], pr[0][rows, ls2], pr[1][rows, ls],
                                    pr[1][rows, ls2], one_m_lb[:, ls], lvl)
                pending.append((starts[r] // c + i, e, front))
                if len(pending) > HGRN_SKEW * nh:
                    finish(*pending.pop(0))
            if r + 1 < n_groups and i < 2:
                nxt.append(project(r + 1, i))
    for item in pending:
        finish(*item)


def _hgrn_mixer(h, w_in, lb_logits, gn, layer, bsz, length, col0):
    t, d = h.shape
    depth, dh = lb_logits.shape
    dk = dh // HGRN_HEADS
    nh = HGRN_HEADS_PER_STEP
    wd = nh * dk
    c = min(HGRN_CHUNK, length)
    assert HGRN_HEADS % nh == 0 and col0 % nh == 0

    def wcol(gi):
        return _wtile(d, wd, layer, lambda hp: (col0 + gi * HGRN_HEADS) // nh + hp)

    return pl.pallas_call(
        functools.partial(_hgrn_kernel, layer, c, min(HGRN_ROW_GROUP, length), nh),
        grid=(HGRN_HEADS // nh, bsz),
        in_specs=[pl.BlockSpec((length, d), lambda hp, b: (b, 0)),
                  wcol(0), wcol(1), wcol(2), wcol(3),
                  pl.BlockSpec((depth, wd), lambda hp, b: (0, hp)),
                  pl.BlockSpec((1, wd), lambda hp, b: (0, hp)),
                  pl.BlockSpec((c, c), lambda hp, b: (0, 0))],
        out_specs=pl.BlockSpec((length, wd), lambda hp, b: (b, hp)),
        out_shape=jax.ShapeDtypeStruct((t, dh), BF16),
        scratch_shapes=[pltpu.VMEM((d, 4 * wd), BF16)],
        compiler_params=_params("arbitrary", "arbitrary"),
        name="hgrn_mixer",
    )(h, w_in, w_in, w_in, w_in, lb_logits, gn.reshape(1, dh), _level_table(c))


def kernel(x, norm_mix, w_in, conv_w, gn_conv, lb_logits, gn_hgrn, w_out, norm_ffn,
           w_gate, w_up, w_down, norm_final):
    bsz, length, d = x.shape
    depth = w_in.shape[0]
    dc = conv_w.shape[-1]
    dh = lb_logits.shape[-1]
    t = bsz * length
    hgrn_col0 = (3 * dc) // (dh // HGRN_HEADS)

    xr = x.reshape(t, d)
    h = _norm(xr, norm_mix[0], BF16)
    for l in range(depth):
        conv_out, w_out_bf, w_down_bf = _conv_mixer(h, w_in, conv_w[l], gn_conv[l], w_out, w_down,
                                                    l, bsz, length)
        hgrn_out = _hgrn_mixer(h, w_in, lb_logits, gn_hgrn[l], l, bsz, length, hgrn_col0)
        xr, h = _out_proj(conv_out, hgrn_out, w_out_bf, xr, norm_ffn[l])
        act = _glu(h, w_gate, w_up, l)
        last = l == depth - 1
        if last:
            y = _down_proj(act, w_down_bf, xr, norm_final, True)
        else:
            xr, h = _down_proj(act, w_down_bf, xr, norm_mix[l + 1], False)
    return y.reshape(bsz, length, d)
```

```python
import functools

import jax
import jax.numpy as jnp
from jax import lax
from jax.experimental import pallas as pl
from jax.experimental.pallas import tpu as pltpu

EPS = 1e-6
MIN_FORGET = 1e-30
LOG2E = 1.4426950408889634
CONV_GROUPS = 8
HGRN_HEADS = 8
LANES = 128
SUBLANES = 8
BF16_ROWS = 16
VMEM_LIMIT = 56 * 1024 * 1024
HGRN_CHUNK = 128
HGRN_ROW_GROUP = 512
HGRN_HEADS_PER_STEP = 2
HGRN_SKEW = 1

NORM_ROWS = 512
GLU_ROWS, GLU_COLS = 1024, 512
OUT_ROWS = 512
DOWN_ROWS = 256
CONV_COLS, CONV_ROWS = 512, 512

F32 = jnp.float32
BF16 = jnp.bfloat16


def _params(*sem):
    return pltpu.CompilerParams(dimension_semantics=sem, vmem_limit_bytes=VMEM_LIMIT)


def _pick(n, pref):
    t = min(pref, n)
    while n % t:
        t -= LANES
    return t


def _rms(xf, gain):
    ms = jnp.mean(xf * xf, axis=-1, keepdims=True)
    return xf * lax.rsqrt(ms + EPS) * gain


def _silu(x):
    return x * (1.0 / (1.0 + jnp.exp2(x * (-LOG2E))))


def _dot_t(a, b):
    return lax.dot_general(a, b, (((1,), (1,)), ((), ())), preferred_element_type=F32)


def _wtile(k, tn, layer, col, buffers=1):
    return pl.BlockSpec((None, k, tn), lambda j, i: (layer, 0, col(j)),
                        pipeline_mode=pl.Buffered(buffers))


def _norm_kernel(x_ref, g_ref, o_ref):
    o_ref[...] = _rms(x_ref[...], g_ref[...]).astype(o_ref.dtype)


def _norm(x, gain, out_dtype):
    t, d = x.shape
    tm = _pick(t, NORM_ROWS)
    return pl.pallas_call(
        _norm_kernel,
        grid=(t // tm,),
        in_specs=[pl.BlockSpec((tm, d), lambda i: (i, 0)),
                  pl.BlockSpec((1, d), lambda i: (0, 0))],
        out_specs=pl.BlockSpec((tm, d), lambda i: (i, 0)),
        out_shape=jax.ShapeDtypeStruct((t, d), out_dtype),
        compiler_params=_params("parallel"),
        name="rms_norm",
    )(x, gain.reshape(1, d))


def _glu_kernel(a_ref, wg_ref, wu_ref, o_ref, wgb_ref, wub_ref):
    @pl.when(pl.program_id(1) == 0)
    def _():
        wgb_ref[...] = wg_ref[...].astype(BF16)
        wub_ref[...] = wu_ref[...].astype(BF16)

    a = a_ref[...]
    g = jnp.dot(a, wgb_ref[...], preferred_element_type=F32)
    u = jnp.dot(a, wub_ref[...], preferred_element_type=F32)
    o_ref[...] = (_silu(g) * u).astype(o_ref.dtype)


def _side_cast_specs(w, layer, steps, inner):
    rows, cols = w.shape[1:]
    rb = rows // steps
    assert rows % steps == 0 and rb % BF16_ROWS == 0
    return (pl.BlockSpec((None, rb, cols), lambda j, i: (layer, j * inner + i, 0)),
            pl.BlockSpec((rb, cols), lambda j, i: (j * inner + i, 0)),
            jax.ShapeDtypeStruct((rows, cols), BF16))


def _glu(a, wg, wu, layer):
    t, k = a.shape
    n = wg.shape[2]
    tm, tn = _pick(t, GLU_ROWS), _pick(n, GLU_COLS)
    return pl.pallas_call(
        _glu_kernel,
        grid=(n // tn, t // tm),
        in_specs=[pl.BlockSpec((tm, k), lambda j, i: (i, 0)),
                  _wtile(k, tn, layer, lambda j: j, buffers=2),
                  _wtile(k, tn, layer, lambda j: j, buffers=2)],
        out_specs=pl.BlockSpec((tm, tn), lambda j, i: (i, j)),
        out_shape=jax.ShapeDtypeStruct((t, n), BF16),
        scratch_shapes=[pltpu.VMEM((k, tn), BF16), pltpu.VMEM((k, tn), BF16)],
        compiler_params=_params("arbitrary", "arbitrary"),
        name="ffn_glu",
    )(a, wg, wu)


def _out_kernel(a1_ref, a2_ref, w1_ref, w2_ref, r_ref, g_ref, x_ref, h_ref):
    acc = jnp.dot(a1_ref[...], w1_ref[...], preferred_element_type=F32)
    acc = acc + jnp.dot(a2_ref[...], w2_ref[...], preferred_element_type=F32)
    xn = r_ref[...] + acc
    x_ref[...] = xn
    h_ref[...] = _rms(xn, g_ref[...]).astype(h_ref.dtype)


def _out_proj(a1, a2, w, res, gain):
    t, k1 = a1.shape
    k2 = a2.shape[1]
    d = w.shape[1]
    assert k1 == k2
    tm = _pick(t, OUT_ROWS)
    return pl.pallas_call(
        _out_kernel,
        grid=(t // tm,),
        in_specs=[pl.BlockSpec((tm, k1), lambda i: (i, 0)),
                  pl.BlockSpec((tm, k2), lambda i: (i, 0)),
                  pl.BlockSpec((k1, d), lambda i: (0, 0)),
                  pl.BlockSpec((k2, d), lambda i: (1, 0)),
                  pl.BlockSpec((tm, d), lambda i: (i, 0)),
                  pl.BlockSpec((1, d), lambda i: (0, 0))],
        out_specs=[pl.BlockSpec((tm, d), lambda i: (i, 0)),
                   pl.BlockSpec((tm, d), lambda i: (i, 0))],
        out_shape=[jax.ShapeDtypeStruct((t, d), F32),
                   jax.ShapeDtypeStruct((t, d), BF16)],
        compiler_params=_params("parallel"),
        name="out_proj",
    )(a1, a2, w, w, res, gain.reshape(1, d))


def _down_kernel(emit_x, a_ref, w_ref, r_ref, g_ref, *outs):
    xn = r_ref[...] + jnp.dot(a_ref[...], w_ref[...], preferred_element_type=F32)
    if emit_x:
        outs[0][...] = xn
    outs[-1][...] = _rms(xn, g_ref[...]).astype(outs[-1].dtype)


def _down_proj(a, w, res, gain, last):
    t, k = a.shape
    d = w.shape[1]
    tm = _pick(t, DOWN_ROWS)
    row = pl.BlockSpec((tm, d), lambda i: (i, 0))
    if last:
        out_specs = row
        out_shape = jax.ShapeDtypeStruct((t, d), F32)
    else:
        out_specs = [row, row]
        out_shape = [jax.ShapeDtypeStruct((t, d), F32), jax.ShapeDtypeStruct((t, d), BF16)]
    return pl.pallas_call(
        functools.partial(_down_kernel, not last),
        grid=(t // tm,),
        in_specs=[pl.BlockSpec((tm, k), lambda i: (i, 0)),
                  pl.BlockSpec((k, d), lambda i: (0, 0), pipeline_mode=pl.Buffered(1)),
                  row,
                  pl.BlockSpec((1, d), lambda i: (0, 0))],
        out_specs=out_specs,
        out_shape=out_shape,
        compiler_params=_params("parallel"),
        name="down_proj",
    )(a, w, res, gain.reshape(1, d))


def _conv_kernel(rt, a_ref, wb_in, wc_in, wh_in, cw_ref, gn_ref, s1_ref, s2_ref,
                 o_ref, s1_o_ref, s2_o_ref, wb_ref):
    length, cw = o_ref.shape
    s1_o_ref[...] = s1_ref[...].astype(BF16)
    s2_o_ref[...] = s2_ref[...].astype(BF16)

    @pl.when(pl.program_id(1) == 0)
    def _():
        for gi, w_ref in enumerate((wb_in, wc_in, wh_in)):
            wb_ref[:, gi * cw:(gi + 1) * cw] = w_ref[...].astype(BF16)

    w = cw_ref[...]
    gd = LANES
    def project(r):
        return jnp.dot(a_ref[r * rt:(r + 1) * rt, :], wb_ref[...], preferred_element_type=F32)

    tail = jnp.zeros((SUBLANES, cw), F32)
    n_groups = length // rt
    nxt = project(0)
    for r in range(n_groups):
        rows = slice(r * rt, (r + 1) * rt)
        pr = nxt
        if r + 1 < n_groups:
            nxt = project(r + 1)
        u = pr[:, cw:2 * cw] * pr[:, 2 * cw:]
        full = jnp.concatenate([tail, u], axis=0)
        u1 = pltpu.roll(full, 1, axis=0)[SUBLANES:]
        u2 = pltpu.roll(full, 2, axis=0)[SUBLANES:]
        tail = u[rt - SUBLANES:]
        y = pr[:, :cw] * (u2 * w[0:1] + u1 * w[1:2] + u * w[2:3])
        for gi in range(cw // gd):
            sl = slice(gi * gd, (gi + 1) * gd)
            o_ref[rows, sl] = _rms(y[:, sl], gn_ref[:, sl]).astype(o_ref.dtype)


def _conv_mixer(h, w_in, conv_w, gn, w_side1, w_side2, layer, bsz, length):
    t, d = h.shape
    dc = conv_w.shape[1]
    assert dc // CONV_GROUPS == LANES
    cw = _pick(dc, CONV_COLS)
    rt = _pick(length, CONV_ROWS)
    nj = dc // cw
    s1_in, s1_out, s1_shape = _side_cast_specs(w_side1, layer, nj * bsz, bsz)
    s2_in, s2_out, s2_shape = _side_cast_specs(w_side2, layer, nj * bsz, bsz)
    return pl.pallas_call(
        functools.partial(_conv_kernel, rt),
        grid=(nj, bsz),
        in_specs=[pl.BlockSpec((length, d), lambda j, b: (b, 0)),
                  _wtile(d, cw, layer, lambda j: j),
                  _wtile(d, cw, layer, lambda j: nj + j),
                  _wtile(d, cw, layer, lambda j: 2 * nj + j),
                  pl.BlockSpec((conv_w.shape[0], cw), lambda j, b: (0, j)),
                  pl.BlockSpec((1, cw), lambda j, b: (0, j)),
                  s1_in, s2_in],
        out_specs=[pl.BlockSpec((length, cw), lambda j, b: (b, j)), s1_out, s2_out],
        out_shape=[jax.ShapeDtypeStruct((t, dc), BF16), s1_shape, s2_shape],
        scratch_shapes=[pltpu.VMEM((d, 3 * cw), BF16)],
        compiler_params=_params("arbitrary", "arbitrary"),
        name="conv_mixer",
    )(h, w_in, w_in, w_in, conv_w, gn.reshape(1, dc), w_side1, w_side2)


def _level_table(c):
    t = jnp.arange(c, dtype=jnp.int32)[:, None]
    s = jnp.arange(c, dtype=jnp.int32)[None, :]
    x = t ^ s
    lvl = jnp.zeros((c, c), jnp.int32)
    m = 1
    while m < c:
        lvl = jnp.where(x >= m, m, lvl)
        m *= 2
    return jnp.where(t > s, lvl, 0)


def _tiles(x):
    return [x[j * SUBLANES:(j + 1) * SUBLANES, :] for j in range(x.shape[0] // SUBLANES)]


def _bcast_row(x, r):
    return jnp.broadcast_to(x[r:r + 1, :], x.shape)


def _hgrn_gates(q, z, v, g, one_m_lb):
    c, dk = q.shape
    nv = c // SUBLANES
    sub = lax.broadcasted_iota(jnp.int32, (SUBLANES, dk), 0)

    qf = _silu(q)
    k = one_m_lb * (1.0 / (1.0 + jnp.exp2(z * LOG2E)))
    fc = jnp.maximum(1.0 - k, MIN_FORGET)
    lf = jnp.log2(fc)

    qf_t, k_t, fc_t = _tiles(qf), _tiles(k), _tiles(fc)

    loc_t = []
    for x in _tiles(lf):
        for s in (1, 2, 4):
            x = x + jnp.where(sub >= s, pltpu.roll(x, s, axis=0), 0.0)
        loc_t.append(x)
    b_t = [loc_t[0]]
    for j in range(1, nv):
        b_t.append(loc_t[j] + _bcast_row(b_t[j - 1], SUBLANES - 1))
    b = jnp.concatenate(b_t, axis=0)
    b_last = b_t[nv - 1][SUBLANES - 1:SUBLANES, :]

    def level_operand(m):
        p_t = []
        if m == 1:
            odd = (sub & 1) == 1
            for j in range(nv):
                p_t.append(jnp.where(odd, qf_t[j] * fc_t[j], k_t[j]))
        elif m < SUBLANES:
            is_q = (sub & m) != 0
            for j in range(nv):
                x = loc_t[j]
                if m == 2:
                    ref = jnp.where(sub < 4, _bcast_row(x, 1), _bcast_row(x, 5))
                else:
                    ref = _bcast_row(x, 3)
                em = jnp.exp2(-jnp.abs(x - ref))
                p_t.append(jnp.where(is_q, qf_t[j], k_t[j]) * em)
        else:
            mv = m // SUBLANES
            for j0 in range(0, nv, 2 * mv):
                ref = _bcast_row(b_t[j0 + mv - 1], SUBLANES - 1)
                for j in range(j0, j0 + mv):
                    p_t.append(k_t[j] * jnp.exp2(ref - b_t[j]))
                for j in range(j0 + mv, j0 + 2 * mv):
                    p_t.append(qf_t[j] * jnp.exp2(b_t[j] - ref))
        return jnp.concatenate(p_t, axis=0).astype(BF16)

    levels = [1 << i for i in range(c.bit_length() - 1)]
    operands = [(levels[i:i + 2], [level_operand(m) for m in levels[i:i + 2]])
                for i in range(0, len(levels), 2)]
    diag = jnp.sum(qf * k, axis=-1, keepdims=True)
    qe = (qf * jnp.exp2(b)).astype(BF16)
    khat_t = (k * jnp.exp2(b_last - b)).astype(BF16).T
    decay = jnp.transpose(jnp.broadcast_to(jnp.exp2(b_last), (dk, dk)))
    return operands, qe, khat_t, v.astype(BF16), diag * v, decay, _silu(g)


def _hgrn_scores(gates, lvl):
    operands, qe, khat_t, v_bf, diag_v, decay, gate = gates
    c, dk = qe.shape
    zero = jnp.zeros((c, dk), BF16)
    scores = jnp.zeros((c, c), F32)
    for ms, ps in operands:
        if len(ps) == 2:
            lhs = jnp.concatenate(ps, axis=1)
            rhs = jnp.concatenate([jnp.concatenate([ps[0], zero], axis=1),
                                   jnp.concatenate([zero, ps[1]], axis=1)], axis=0)
            sp = _dot_t(lhs, rhs)
        else:
            sp = _dot_t(ps[0], ps[0])
        for j, m in enumerate(ms):
            scores = jnp.where(lvl == m, sp[:, j * c:(j + 1) * c], scores)
    lhs = jnp.concatenate([jnp.concatenate([qe, scores.astype(BF16)], axis=1),
                           jnp.concatenate([jnp.zeros((dk, dk), BF16), khat_t], axis=1)], axis=0)
    return lhs, v_bf, diag_v, decay, gate


def _hgrn_back(front, gn, st):
    lhs, v_bf, diag_v, decay, gate = front
    c = v_bf.shape[0]
    rhs = jnp.concatenate([st.astype(BF16), v_bf], axis=0)
    res = jnp.dot(lhs, rhs, preferred_element_type=F32)
    out = _rms(res[:c] + diag_v, gn) * gate
    return out, st * decay + res[c:]


def _hgrn_kernel(layer, c, rg, nh, a_ref, wq_in, wz_in, wv_in, wg_in, lbl_ref, gn_ref, lvl_ref,
                 o_ref, wb_ref):
    length, wd = o_ref.shape
    dk = wd // nh

    @pl.when(pl.program_id(1) == 0)
    def _():
        for gi, w_ref in enumerate((wq_in, wz_in, wv_in, wg_in)):
            wb_ref[:, gi * wd:(gi + 1) * wd] = w_ref[...].astype(BF16)

    lg = lbl_ref[...]
    ex = jnp.exp(lg - jnp.max(lg, axis=0, keepdims=True))
    p = ex / jnp.sum(ex, axis=0, keepdims=True)
    cum = jnp.sum(p[:layer + 1], axis=0, keepdims=True)
    lb = jnp.clip(cum - p[0:1], 0.0, 1.0 - 1e-4)
    one_m_lb = 1.0 - lb
    gn = gn_ref[...]
    lvl = lvl_ref[...]

    if length >= 4 * rg:
        sizes = [rg // 2] + [rg] * (length // rg - 1) + [rg // 2]
    else:
        sizes = [rg] * (length // rg)
    starts = [sum(sizes[:r]) for r in range(len(sizes))]

    pw = 2 * LANES
    n_pieces = 4 * wd // pw

    def project(r, j):
        return jnp.dot(a_ref[starts[r]:starts[r] + sizes[r], :], wb_ref[:, j * pw:(j + 1) * pw],
                       preferred_element_type=F32)

    def columns(pr, gi, e, rows):
        lo = gi * wd + e * dk
        return pr[lo // pw][rows, lo % pw:lo % pw + dk]

    def finish(ci, e, front):
        ls = slice(e * dk, (e + 1) * dk)
        out, sts[e] = _hgrn_back(front, gn[:, ls], sts[e])
        o_ref[ci * c:(ci + 1) * c, ls] = out.astype(o_ref.dtype)

    sts = [jnp.zeros((dk, dk), F32) for _ in range(nh)]
    n_groups = len(sizes)
    nxt = [project(0, j) for j in range(n_pieces)]
    gated, scored = [], []
    for r in range(n_groups):
        pr = nxt
        nxt = []
        for i in range(sizes[r] // c):
            rows = slice(i * c, (i + 1) * c)
            for e in range(nh):
                if r + 1 < n_groups and len(nxt) < n_pieces:
                    nxt.append(project(r + 1, len(nxt)))
                ls = slice(e * dk, (e + 1) * dk)
                gates = _hgrn_gates(*(columns(pr, gi, e, rows) for gi in range(4)), one_m_lb[:, ls])
                gated.append((starts[r] // c + i, e, gates))
                if len(gated) > HGRN_SKEW * nh:
                    ci, ce, gt = gated.pop(0)
                    scored.append((ci, ce, _hgrn_scores(gt, lvl)))
                if len(scored) > HGRN_SKEW * nh:
                    finish(*scored.pop(0))
        while r + 1 < n_groups and len(nxt) < n_pieces:
            nxt.append(project(r + 1, len(nxt)))
    for ci, ce, gt in gated:
        scored.append((ci, ce, _hgrn_scores(gt, lvl)))
        if len(scored) > HGRN_SKEW * nh:
            finish(*scored.pop(0))
    for item in scored:
        finish(*item)


def _hgrn_mixer(h, w_in, lb_logits, gn, layer, bsz, length, col0):
    t, d = h.shape
    depth, dh = lb_logits.shape
    dk = dh // HGRN_HEADS
    nh = HGRN_HEADS_PER_STEP
    wd = nh * dk
    c = min(HGRN_CHUNK, length)
    assert HGRN_HEADS % nh == 0 and col0 % nh == 0

    def wcol(gi):
        return _wtile(d, wd, layer, lambda hp: (col0 + gi * HGRN_HEADS) // nh + hp)

    return pl.pallas_call(
        functools.partial(_hgrn_kernel, layer, c, min(HGRN_ROW_GROUP, length), nh),
        grid=(HGRN_HEADS // nh, bsz),
        in_specs=[pl.BlockSpec((length, d), lambda hp, b: (b, 0)),
                  wcol(0), wcol(1), wcol(2), wcol(3),
                  pl.BlockSpec((depth, wd), lambda hp, b: (0, hp)),
                  pl.BlockSpec((1, wd), lambda hp, b: (0, hp)),
                  pl.BlockSpec((c, c), lambda hp, b: (0, 0))],
        out_specs=pl.BlockSpec((length, wd), lambda hp, b: (b, hp)),
        out_shape=jax.ShapeDtypeStruct((t, dh), BF16),
        scratch_shapes=[pltpu.VMEM((d, 4 * wd), BF16)],
        compiler_params=_params("arbitrary", "arbitrary"),
        name="hgrn_mixer",
    )(h, w_in, w_in, w_in, w_in, lb_logits, gn.reshape(1, dh), _level_table(c))


def kernel(x, norm_mix, w_in, conv_w, gn_conv, lb_logits, gn_hgrn, w_out, norm_ffn,
           w_gate, w_up, w_down, norm_final):
    bsz, length, d = x.shape
    depth = w_in.shape[0]
    dc = conv_w.shape[-1]
    dh = lb_logits.shape[-1]
    t = bsz * length
    hgrn_col0 = (3 * dc) // (dh // HGRN_HEADS)

    xr = x.reshape(t, d)
    h = _norm(xr, norm_mix[0], BF16)
    for l in range(depth):
        conv_out, w_out_bf, w_down_bf = _conv_mixer(h, w_in, conv_w[l], gn_conv[l], w_out, w_down,
                                                    l, bsz, length)
        hgrn_out = _hgrn_mixer(h, w_in, lb_logits, gn_hgrn[l], l, bsz, length, hgrn_col0)
        xr, h = _out_proj(conv_out, hgrn_out, w_out_bf, xr, norm_ffn[l])
        act = _glu(h, w_gate, w_up, l)
        last = l == depth - 1
        if last:
            y = _down_proj(act, w_down_bf, xr, norm_final, True)
        else:
            xr, h = _down_proj(act, w_down_bf, xr, norm_mix[l + 1], False)
    return y.reshape(bsz, length, d)
```

```python
import functools

import jax
import jax.numpy as jnp
from jax import lax
from jax.experimental import pallas as pl
from jax.experimental.pallas import tpu as pltpu

EPS = 1e-6
MIN_FORGET = 1e-30
LOG2E = 1.4426950408889634
CONV_GROUPS = 8
HGRN_HEADS = 8
LANES = 128
SUBLANES = 8
BF16_ROWS = 16
VMEM_LIMIT = 56 * 1024 * 1024
HGRN_CHUNK = 128
HGRN_ROW_GROUP = 512
HGRN_HEADS_PER_STEP = 2
HGRN_SKEW = 4

NORM_ROWS = 512
GLU_ROWS, GLU_COLS = 1024, 512
OUT_ROWS = 512
DOWN_ROWS = 256
CONV_COLS, CONV_ROWS = 512, 512

F32 = jnp.float32
BF16 = jnp.bfloat16


def _params(*sem):
    return pltpu.CompilerParams(dimension_semantics=sem, vmem_limit_bytes=VMEM_LIMIT)


def _pick(n, pref):
    t = min(pref, n)
    while n % t:
        t -= LANES
    return t


def _rms(xf, gain):
    ms = jnp.mean(xf * xf, axis=-1, keepdims=True)
    return xf * lax.rsqrt(ms + EPS) * gain


def _silu(x):
    return x * (1.0 / (1.0 + jnp.exp2(x * (-LOG2E))))


def _dot_t(a, b):
    return lax.dot_general(a, b, (((1,), (1,)), ((), ())), preferred_element_type=F32)


def _wtile(k, tn, layer, col, buffers=1):
    return pl.BlockSpec((None, k, tn), lambda j, i: (layer, 0, col(j)),
                        pipeline_mode=pl.Buffered(buffers))


def _norm_kernel(x_ref, g_ref, o_ref):
    o_ref[...] = _rms(x_ref[...], g_ref[...]).astype(o_ref.dtype)


def _norm(x, gain, out_dtype):
    t, d = x.shape
    tm = _pick(t, NORM_ROWS)
    return pl.pallas_call(
        _norm_kernel,
        grid=(t // tm,),
        in_specs=[pl.BlockSpec((tm, d), lambda i: (i, 0)),
                  pl.BlockSpec((1, d), lambda i: (0, 0))],
        out_specs=pl.BlockSpec((tm, d), lambda i: (i, 0)),
        out_shape=jax.ShapeDtypeStruct((t, d), out_dtype),
        compiler_params=_params("parallel"),
        name="rms_norm",
    )(x, gain.reshape(1, d))


def _glu_kernel(a_ref, wg_ref, wu_ref, o_ref, wgb_ref, wub_ref):
    @pl.when(pl.program_id(1) == 0)
    def _():
        wgb_ref[...] = wg_ref[...].astype(BF16)
        wub_ref[...] = wu_ref[...].astype(BF16)

    a = a_ref[...]
    g = jnp.dot(a, wgb_ref[...], preferred_element_type=F32)
    u = jnp.dot(a, wub_ref[...], preferred_element_type=F32)
    o_ref[...] = (_silu(g) * u).astype(o_ref.dtype)


def _side_cast_specs(w, layer, steps, inner):
    rows, cols = w.shape[1:]
    rb = rows // steps
    assert rows % steps == 0 and rb % BF16_ROWS == 0
    return (pl.BlockSpec((None, rb, cols), lambda j, i: (layer, j * inner + i, 0)),
            pl.BlockSpec((rb, cols), lambda j, i: (j * inner + i, 0)),
            jax.ShapeDtypeStruct((rows, cols), BF16))


def _glu(a, wg, wu, layer):
    t, k = a.shape
    n = wg.shape[2]
    tm, tn = _pick(t, GLU_ROWS), _pick(n, GLU_COLS)
    return pl.pallas_call(
        _glu_kernel,
        grid=(n // tn, t // tm),
        in_specs=[pl.BlockSpec((tm, k), lambda j, i: (i, 0)),
                  _wtile(k, tn, layer, lambda j: j, buffers=2),
                  _wtile(k, tn, layer, lambda j: j, buffers=2)],
        out_specs=pl.BlockSpec((tm, tn), lambda j, i: (i, j)),
        out_shape=jax.ShapeDtypeStruct((t, n), BF16),
        scratch_shapes=[pltpu.VMEM((k, tn), BF16), pltpu.VMEM((k, tn), BF16)],
        compiler_params=_params("arbitrary", "arbitrary"),
        name="ffn_glu",
    )(a, wg, wu)


def _out_kernel(a1_ref, a2_ref, w1_ref, w2_ref, r_ref, g_ref, x_ref, h_ref):
    acc = jnp.dot(a1_ref[...], w1_ref[...], preferred_element_type=F32)
    acc = acc + jnp.dot(a2_ref[...], w2_ref[...], preferred_element_type=F32)
    xn = r_ref[...] + acc
    x_ref[...] = xn
    h_ref[...] = _rms(xn, g_ref[...]).astype(h_ref.dtype)


def _out_proj(a1, a2, w, res, gain):
    t, k1 = a1.shape
    k2 = a2.shape[1]
    d = w.shape[1]
    assert k1 == k2
    tm = _pick(t, OUT_ROWS)
    return pl.pallas_call(
        _out_kernel,
        grid=(t // tm,),
        in_specs=[pl.BlockSpec((tm, k1), lambda i: (i, 0)),
                  pl.BlockSpec((tm, k2), lambda i: (i, 0)),
                  pl.BlockSpec((k1, d), lambda i: (0, 0)),
                  pl.BlockSpec((k2, d), lambda i: (1, 0)),
                  pl.BlockSpec((tm, d), lambda i: (i, 0)),
                  pl.BlockSpec((1, d), lambda i: (0, 0))],
        out_specs=[pl.BlockSpec((tm, d), lambda i: (i, 0)),
                   pl.BlockSpec((tm, d), lambda i: (i, 0))],
        out_shape=[jax.ShapeDtypeStruct((t, d), F32),
                   jax.ShapeDtypeStruct((t, d), BF16)],
        compiler_params=_params("parallel"),
        name="out_proj",
    )(a1, a2, w, w, res, gain.reshape(1, d))


def _down_kernel(emit_x, a_ref, w_ref, r_ref, g_ref, *outs):
    xn = r_ref[...] + jnp.dot(a_ref[...], w_ref[...], preferred_element_type=F32)
    if emit_x:
        outs[0][...] = xn
    outs[-1][...] = _rms(xn, g_ref[...]).astype(outs[-1].dtype)


def _down_proj(a, w, res, gain, last):
    t, k = a.shape
    d = w.shape[1]
    tm = _pick(t, DOWN_ROWS)
    row = pl.BlockSpec((tm, d), lambda i: (i, 0))
    if last:
        out_specs = row
        out_shape = jax.ShapeDtypeStruct((t, d), F32)
    else:
        out_specs = [row, row]
        out_shape = [jax.ShapeDtypeStruct((t, d), F32), jax.ShapeDtypeStruct((t, d), BF16)]
    return pl.pallas_call(
        functools.partial(_down_kernel, not last),
        grid=(t // tm,),
        in_specs=[pl.BlockSpec((tm, k), lambda i: (i, 0)),
                  pl.BlockSpec((k, d), lambda i: (0, 0), pipeline_mode=pl.Buffered(1)),
                  row,
                  pl.BlockSpec((1, d), lambda i: (0, 0))],
        out_specs=out_specs,
        out_shape=out_shape,
        compiler_params=_params("parallel"),
        name="down_proj",
    )(a, w, res, gain.reshape(1, d))


def _conv_kernel(rt, a_ref, wb_in, wc_in, wh_in, cw_ref, gn_ref, s1_ref, s2_ref,
                 o_ref, s1_o_ref, s2_o_ref, wb_ref):
    length, cw = o_ref.shape
    s1_o_ref[...] = s1_ref[...].astype(BF16)
    s2_o_ref[...] = s2_ref[...].astype(BF16)

    @pl.when(pl.program_id(1) == 0)
    def _():
        for gi, w_ref in enumerate((wb_in, wc_in, wh_in)):
            wb_ref[:, gi * cw:(gi + 1) * cw] = w_ref[...].astype(BF16)

    w = cw_ref[...]
    gd = LANES
    def project(r):
        return jnp.dot(a_ref[r * rt:(r + 1) * rt, :], wb_ref[...], preferred_element_type=F32)

    tail = jnp.zeros((SUBLANES, cw), F32)
    n_groups = length // rt
    nxt = project(0)
    for r in range(n_groups):
        rows = slice(r * rt, (r + 1) * rt)
        pr = nxt
        if r + 1 < n_groups:
            nxt = project(r + 1)
        u = pr[:, cw:2 * cw] * pr[:, 2 * cw:]
        full = jnp.concatenate([tail, u], axis=0)
        u1 = pltpu.roll(full, 1, axis=0)[SUBLANES:]
        u2 = pltpu.roll(full, 2, axis=0)[SUBLANES:]
        tail = u[rt - SUBLANES:]
        y = pr[:, :cw] * (u2 * w[0:1] + u1 * w[1:2] + u * w[2:3])
        for gi in range(cw // gd):
            sl = slice(gi * gd, (gi + 1) * gd)
            o_ref[rows, sl] = _rms(y[:, sl], gn_ref[:, sl]).astype(o_ref.dtype)


def _conv_mixer(h, w_in, conv_w, gn, w_side1, w_side2, layer, bsz, length):
    t, d = h.shape
    dc = conv_w.shape[1]
    assert dc // CONV_GROUPS == LANES
    cw = _pick(dc, CONV_COLS)
    rt = _pick(length, CONV_ROWS)
    nj = dc // cw
    s1_in, s1_out, s1_shape = _side_cast_specs(w_side1, layer, nj * bsz, bsz)
    s2_in, s2_out, s2_shape = _side_cast_specs(w_side2, layer, nj * bsz, bsz)
    return pl.pallas_call(
        functools.partial(_conv_kernel, rt),
        grid=(nj, bsz),
        in_specs=[pl.BlockSpec((length, d), lambda j, b: (b, 0)),
                  _wtile(d, cw, layer, lambda j: j),
                  _wtile(d, cw, layer, lambda j: nj + j),
                  _wtile(d, cw, layer, lambda j: 2 * nj + j),
                  pl.BlockSpec((conv_w.shape[0], cw), lambda j, b: (0, j)),
                  pl.BlockSpec((1, cw), lambda j, b: (0, j)),
                  s1_in, s2_in],
        out_specs=[pl.BlockSpec((length, cw), lambda j, b: (b, j)), s1_out, s2_out],
        out_shape=[jax.ShapeDtypeStruct((t, dc), BF16), s1_shape, s2_shape],
        scratch_shapes=[pltpu.VMEM((d, 3 * cw), BF16)],
        compiler_params=_params("arbitrary", "arbitrary"),
        name="conv_mixer",
    )(h, w_in, w_in, w_in, conv_w, gn.reshape(1, dc), w_side1, w_side2)


def _level_table(c):
    t = jnp.arange(c, dtype=jnp.int32)[:, None]
    s = jnp.arange(c, dtype=jnp.int32)[None, :]
    x = t ^ s
    lvl = jnp.zeros((c, c), jnp.int32)
    m = 1
    while m < c:
        lvl = jnp.where(x >= m, m, lvl)
        m *= 2
    return jnp.where(t > s, lvl, 0)


def _tiles(x):
    return [x[j * SUBLANES:(j + 1) * SUBLANES, :] for j in range(x.shape[0] // SUBLANES)]


def _bcast_row(x, r):
    return jnp.broadcast_to(x[r:r + 1, :], x.shape)


def _hgrn_gates(q, z, v, g, one_m_lb):
    c, dk = q.shape
    nv = c // SUBLANES
    sub = lax.broadcasted_iota(jnp.int32, (SUBLANES, dk), 0)

    qf = _silu(q)
    k = one_m_lb * (1.0 / (1.0 + jnp.exp2(z * LOG2E)))
    fc = jnp.maximum(1.0 - k, MIN_FORGET)
    lf = jnp.log2(fc)

    qf_t, k_t, fc_t = _tiles(qf), _tiles(k), _tiles(fc)

    loc_t = []
    for x in _tiles(lf):
        for s in (1, 2, 4):
            x = x + jnp.where(sub >= s, pltpu.roll(x, s, axis=0), 0.0)
        loc_t.append(x)
    b_t = [loc_t[0]]
    for j in range(1, nv):
        b_t.append(loc_t[j] + _bcast_row(b_t[j - 1], SUBLANES - 1))
    b = jnp.concatenate(b_t, axis=0)
    b_last = b_t[nv - 1][SUBLANES - 1:SUBLANES, :]

    def level_operand(m):
        p_t = []
        if m == 1:
            odd = (sub & 1) == 1
            for j in range(nv):
                p_t.append(jnp.where(odd, qf_t[j] * fc_t[j], k_t[j]))
        elif m < SUBLANES:
            is_q = (sub & m) != 0
            for j in range(nv):
                x = loc_t[j]
                if m == 2:
                    ref = jnp.where(sub < 4, _bcast_row(x, 1), _bcast_row(x, 5))
                else:
                    ref = _bcast_row(x, 3)
                em = jnp.exp2(-jnp.abs(x - ref))
                p_t.append(jnp.where(is_q, qf_t[j], k_t[j]) * em)
        else:
            mv = m // SUBLANES
            for j0 in range(0, nv, 2 * mv):
                ref = _bcast_row(b_t[j0 + mv - 1], SUBLANES - 1)
                for j in range(j0, j0 + mv):
                    p_t.append(k_t[j] * jnp.exp2(ref - b_t[j]))
                for j in range(j0 + mv, j0 + 2 * mv):
                    p_t.append(qf_t[j] * jnp.exp2(b_t[j] - ref))
        return jnp.concatenate(p_t, axis=0).astype(BF16)

    levels = [1 << i for i in range(c.bit_length() - 1)]
    operands = [(levels[i:i + 2], [level_operand(m) for m in levels[i:i + 2]])
                for i in range(0, len(levels), 2)]
    diag = jnp.sum(qf * k, axis=-1, keepdims=True)
    qe = (qf * jnp.exp2(b)).astype(BF16)
    khat_t = (k * jnp.exp2(b_last - b)).astype(BF16).T
    decay = jnp.transpose(jnp.broadcast_to(jnp.exp2(b_last), (dk, dk)))
    return operands, qe, khat_t, v.astype(BF16), diag * v, decay, _silu(g)


def _hgrn_scores(gates, lvl):
    operands, qe, khat_t, v_bf, diag_v, decay, gate = gates
    c, dk = qe.shape
    zero = jnp.zeros((c, dk), BF16)
    scores = jnp.zeros((c, c), F32)
    for ms, ps in operands:
        if len(ps) == 2:
            lhs = jnp.concatenate(ps, axis=1)
            rhs = jnp.concatenate([jnp.concatenate([ps[0], zero], axis=1),
                                   jnp.concatenate([zero, ps[1]], axis=1)], axis=0)
            sp = _dot_t(lhs, rhs)
        else:
            sp = _dot_t(ps[0], ps[0])
        for j, m in enumerate(ms):
            scores = jnp.where(lvl == m, sp[:, j * c:(j + 1) * c], scores)
    lhs = jnp.concatenate([jnp.concatenate([qe, scores.astype(BF16)], axis=1),
                           jnp.concatenate([jnp.zeros((dk, dk), BF16), khat_t], axis=1)], axis=0)
    return lhs, v_bf, diag_v, decay, gate


def _hgrn_back(front, gn, st):
    lhs, v_bf, diag_v, decay, gate = front
    c = v_bf.shape[0]
    rhs = jnp.concatenate([st.astype(BF16), v_bf], axis=0)
    res = jnp.dot(lhs, rhs, preferred_element_type=F32)
    out = _rms(res[:c] + diag_v, gn) * gate
    return out, st * decay + res[c:]


def _hgrn_kernel(layer, c, rg, nh, a_ref, wq_in, wz_in, wv_in, wg_in, lbl_ref, gn_ref, lvl_ref,
                 o_ref, wb_ref):
    length, wd = o_ref.shape
    dk = wd // nh

    @pl.when(pl.program_id(1) == 0)
    def _():
        for gi, w_ref in enumerate((wq_in, wz_in, wv_in, wg_in)):
            wb_ref[:, gi * wd:(gi + 1) * wd] = w_ref[...].astype(BF16)

    lg = lbl_ref[...]
    ex = jnp.exp(lg - jnp.max(lg, axis=0, keepdims=True))
    p = ex / jnp.sum(ex, axis=0, keepdims=True)
    cum = jnp.sum(p[:layer + 1], axis=0, keepdims=True)
    lb = jnp.clip(cum - p[0:1], 0.0, 1.0 - 1e-4)
    one_m_lb = 1.0 - lb
    gn = gn_ref[...]
    lvl = lvl_ref[...]

    if length >= 4 * rg:
        sizes = [rg // 2] + [rg] * (length // rg - 1) + [rg // 2]
    else:
        sizes = [rg] * (length // rg)
    starts = [sum(sizes[:r]) for r in range(len(sizes))]

    pw = 2 * LANES
    n_pieces = 4 * wd // pw

    def project(r, j):
        return jnp.dot(a_ref[starts[r]:starts[r] + sizes[r], :], wb_ref[:, j * pw:(j + 1) * pw],
                       preferred_element_type=F32)

    def columns(pr, gi, e, rows):
        lo = gi * wd + e * dk
        return pr[lo // pw][rows, lo % pw:lo % pw + dk]

    def finish(ci, e, front):
        ls = slice(e * dk, (e + 1) * dk)
        out, sts[e] = _hgrn_back(front, gn[:, ls], sts[e])
        o_ref[ci * c:(ci + 1) * c, ls] = out.astype(o_ref.dtype)

    sts = [jnp.zeros((dk, dk), F32) for _ in range(nh)]
    n_groups = len(sizes)
    nxt = [project(0, j) for j in range(n_pieces)]
    gated, scored = [], []
    for r in range(n_groups):
        pr = nxt
        nxt = []
        for i in range(sizes[r] // c):
            rows = slice(i * c, (i + 1) * c)
            for e in range(nh):
                if r + 1 < n_groups and len(nxt) < n_pieces:
                    nxt.append(project(r + 1, len(nxt)))
                ls = slice(e * dk, (e + 1) * dk)
                gates = _hgrn_gates(*(columns(pr, gi, e, rows) for gi in range(4)), one_m_lb[:, ls])
                gated.append((starts[r] // c + i, e, gates))
                if len(gated) > HGRN_SKEW * nh:
                    ci, ce, gt = gated.pop(0)
                    scored.append((ci, ce, _hgrn_scores(gt, lvl)))
                if len(scored) > nh:
                    finish(*scored.pop(0))
        while r + 1 < n_groups and len(nxt) < n_pieces:
            nxt.append(project(r + 1, len(nxt)))
    for ci, ce, gt in gated:
        scored.append((ci, ce, _hgrn_scores(gt, lvl)))
        if len(scored) > nh:
            finish(*scored.pop(0))
    for item in scored:
        finish(*item)


def _hgrn_mixer(h, w_in, lb_logits, gn, layer, bsz, length, col0):
    t, d = h.shape
    depth, dh = lb_logits.shape
    dk = dh // HGRN_HEADS
    nh = HGRN_HEADS_PER_STEP
    wd = nh * dk
    c = min(HGRN_CHUNK, length)
    assert HGRN_HEADS % nh == 0 and col0 % nh == 0

    def wcol(gi):
        return _wtile(d, wd, layer, lambda hp: (col0 + gi * HGRN_HEADS) // nh + hp)

    return pl.pallas_call(
        functools.partial(_hgrn_kernel, layer, c, min(HGRN_ROW_GROUP, length), nh),
        grid=(HGRN_HEADS // nh, bsz),
        in_specs=[pl.BlockSpec((length, d), lambda hp, b: (b, 0)),
                  wcol(0), wcol(1), wcol(2), wcol(3),
                  pl.BlockSpec((depth, wd), lambda hp, b: (0, hp)),
                  pl.BlockSpec((1, wd), lambda hp, b: (0, hp)),
                  pl.BlockSpec((c, c), lambda hp, b: (0, 0))],
        out_specs=pl.BlockSpec((length, wd), lambda hp, b: (b, hp)),
        out_shape=jax.ShapeDtypeStruct((t, dh), BF16),
        scratch_shapes=[pltpu.VMEM((d, 4 * wd), BF16)],
        compiler_params=_params("arbitrary", "arbitrary"),
        name="hgrn_mixer",
    )(h, w_in, w_in, w_in, w_in, lb_logits, gn.reshape(1, dh), _level_table(c))


def kernel(x, norm_mix, w_in, conv_w, gn_conv, lb_logits, gn_hgrn, w_out, norm_ffn,
           w_gate, w_up, w_down, norm_final):
    bsz, length, d = x.shape
    depth = w_in.shape[0]
    dc = conv_w.shape[-1]
    dh = lb_logits.shape[-1]
    t = bsz * length
    hgrn_col0 = (3 * dc) // (dh // HGRN_HEADS)

    xr = x.reshape(t, d)
    h = _norm(xr, norm_mix[0], BF16)
    for l in range(depth):
        conv_out, w_out_bf, w_down_bf = _conv_mixer(h, w_in, conv_w[l], gn_conv[l], w_out, w_down,
                                                    l, bsz, length)
        hgrn_out = _hgrn_mixer(h, w_in, lb_logits, gn_hgrn[l], l, bsz, length, hgrn_col0)
        xr, h = _out_proj(conv_out, hgrn_out, w_out_bf, xr, norm_ffn[l])
        act = _glu(h, w_gate, w_up, l)
        last = l == depth - 1
        if last:
            y = _down_proj(act, w_down_bf, xr, norm_final, True)
        else:
            xr, h = _down_proj(act, w_down_bf, xr, norm_mix[l + 1], False)
    return y.reshape(bsz, length, d)
```

```python
import functools

import jax
import jax.numpy as jnp
from jax import lax
from jax.experimental import pallas as pl
from jax.experimental.pallas import tpu as pltpu

EPS = 1e-6
MIN_FORGET = 1e-30
LOG2E = 1.4426950408889634
CONV_GROUPS = 8
HGRN_HEADS = 8
LANES = 128
SUBLANES = 8
BF16_ROWS = 16
VMEM_LIMIT = 56 * 1024 * 1024
HGRN_CHUNK = 128
HGRN_ROW_GROUP = 512
HGRN_HEADS_PER_STEP = 2
HGRN_SKEW = 4

NORM_ROWS = 512
GLU_ROWS, GLU_COLS = 1024, 512
OUT_ROWS = 512
DOWN_ROWS = 256
CONV_COLS, CONV_ROWS = 512, 512

F32 = jnp.float32
BF16 = jnp.bfloat16


def _params(*sem):
    return pltpu.CompilerParams(dimension_semantics=sem, vmem_limit_bytes=VMEM_LIMIT)


def _pick(n, pref):
    t = min(pref, n)
    while n % t:
        t -= LANES
    return t


def _rms(xf, gain):
    ms = jnp.mean(xf * xf, axis=-1, keepdims=True)
    return xf * lax.rsqrt(ms + EPS) * gain


def _silu(x):
    return x * (1.0 / (1.0 + jnp.exp2(x * (-LOG2E))))


def _dot_t(a, b):
    return lax.dot_general(a, b, (((1,), (1,)), ((), ())), preferred_element_type=F32)


def _wtile(k, tn, layer, col, buffers=1):
    return pl.BlockSpec((None, k, tn), lambda j, i: (layer, 0, col(j)),
                        pipeline_mode=pl.Buffered(buffers))


def _norm_kernel(x_ref, g_ref, o_ref):
    o_ref[...] = _rms(x_ref[...], g_ref[...]).astype(o_ref.dtype)


def _norm(x, gain, out_dtype):
    t, d = x.shape
    tm = _pick(t, NORM_ROWS)
    return pl.pallas_call(
        _norm_kernel,
        grid=(t // tm,),
        in_specs=[pl.BlockSpec((tm, d), lambda i: (i, 0)),
                  pl.BlockSpec((1, d), lambda i: (0, 0))],
        out_specs=pl.BlockSpec((tm, d), lambda i: (i, 0)),
        out_shape=jax.ShapeDtypeStruct((t, d), out_dtype),
        compiler_params=_params("parallel"),
        name="rms_norm",
    )(x, gain.reshape(1, d))


def _glu_kernel(a_ref, wg_ref, wu_ref, o_ref, wgb_ref, wub_ref):
    @pl.when(pl.program_id(1) == 0)
    def _():
        wgb_ref[...] = wg_ref[...].astype(BF16)
        wub_ref[...] = wu_ref[...].astype(BF16)

    a = a_ref[...]
    g = jnp.dot(a, wgb_ref[...], preferred_element_type=F32)
    u = jnp.dot(a, wub_ref[...], preferred_element_type=F32)
    o_ref[...] = (_silu(g) * u).astype(o_ref.dtype)


def _side_cast_specs(w, layer, steps, inner):
    rows, cols = w.shape[1:]
    rb = rows // steps
    assert rows % steps == 0 and rb % BF16_ROWS == 0
    return (pl.BlockSpec((None, rb, cols), lambda j, i: (layer, j * inner + i, 0)),
            pl.BlockSpec((rb, cols), lambda j, i: (j * inner + i, 0)),
            jax.ShapeDtypeStruct((rows, cols), BF16))


def _glu(a, wg, wu, layer):
    t, k = a.shape
    n = wg.shape[2]
    tm, tn = _pick(t, GLU_ROWS), _pick(n, GLU_COLS)
    return pl.pallas_call(
        _glu_kernel,
        grid=(n // tn, t // tm),
        in_specs=[pl.BlockSpec((tm, k), lambda j, i: (i, 0)),
                  _wtile(k, tn, layer, lambda j: j, buffers=2),
                  _wtile(k, tn, layer, lambda j: j, buffers=2)],
        out_specs=pl.BlockSpec((tm, tn), lambda j, i: (i, j)),
        out_shape=jax.ShapeDtypeStruct((t, n), BF16),
        scratch_shapes=[pltpu.VMEM((k, tn), BF16), pltpu.VMEM((k, tn), BF16)],
        compiler_params=_params("arbitrary", "arbitrary"),
        name="ffn_glu",
    )(a, wg, wu)


def _out_kernel(a1_ref, a2_ref, w1_ref, w2_ref, r_ref, g_ref, x_ref, h_ref):
    acc = jnp.dot(a1_ref[...], w1_ref[...], preferred_element_type=F32)
    acc = acc + jnp.dot(a2_ref[...], w2_ref[...], preferred_element_type=F32)
    xn = r_ref[...] + acc
    x_ref[...] = xn
    h_ref[...] = _rms(xn, g_ref[...]).astype(h_ref.dtype)


def _out_proj(a1, a2, w, res, gain):
    t, k1 = a1.shape
    k2 = a2.shape[1]
    d = w.shape[1]
    assert k1 == k2
    tm = _pick(t, OUT_ROWS)
    return pl.pallas_call(
        _out_kernel,
        grid=(t // tm,),
        in_specs=[pl.BlockSpec((tm, k1), lambda i: (i, 0)),
                  pl.BlockSpec((tm, k2), lambda i: (i, 0)),
                  pl.BlockSpec((k1, d), lambda i: (0, 0)),
                  pl.BlockSpec((k2, d), lambda i: (1, 0)),
                  pl.BlockSpec((tm, d), lambda i: (i, 0)),
                  pl.BlockSpec((1, d), lambda i: (0, 0))],
        out_specs=[pl.BlockSpec((tm, d), lambda i: (i, 0)),
                   pl.BlockSpec((tm, d), lambda i: (i, 0))],
        out_shape=[jax.ShapeDtypeStruct((t, d), F32),
                   jax.ShapeDtypeStruct((t, d), BF16)],
        compiler_params=_params("parallel"),
        name="out_proj",
    )(a1, a2, w, w, res, gain.reshape(1, d))


def _down_kernel(emit_x, a_ref, w_ref, r_ref, g_ref, *outs):
    xn = r_ref[...] + jnp.dot(a_ref[...], w_ref[...], preferred_element_type=F32)
    if emit_x:
        outs[0][...] = xn
    outs[-1][...] = _rms(xn, g_ref[...]).astype(outs[-1].dtype)


def _down_proj(a, w, res, gain, last):
    t, k = a.shape
    d = w.shape[1]
    tm = _pick(t, DOWN_ROWS)
    row = pl.BlockSpec((tm, d), lambda i: (i, 0))
    if last:
        out_specs = row
        out_shape = jax.ShapeDtypeStruct((t, d), F32)
    else:
        out_specs = [row, row]
        out_shape = [jax.ShapeDtypeStruct((t, d), F32), jax.ShapeDtypeStruct((t, d), BF16)]
    return pl.pallas_call(
        functools.partial(_down_kernel, not last),
        grid=(t // tm,),
        in_specs=[pl.BlockSpec((tm, k), lambda i: (i, 0)),
                  pl.BlockSpec((k, d), lambda i: (0, 0), pipeline_mode=pl.Buffered(1)),
                  row,
                  pl.BlockSpec((1, d), lambda i: (0, 0))],
        out_specs=out_specs,
        out_shape=out_shape,
        compiler_params=_params("parallel"),
        name="down_proj",
    )(a, w, res, gain.reshape(1, d))


def _conv_kernel(rt, a_ref, wb_in, wc_in, wh_in, cw_ref, gn_ref, s1_ref, s2_ref,
                 o_ref, s1_o_ref, s2_o_ref, wb_ref):
    length, cw = o_ref.shape
    s1_o_ref[...] = s1_ref[...].astype(BF16)
    s2_o_ref[...] = s2_ref[...].astype(BF16)

    @pl.when(pl.program_id(1) == 0)
    def _():
        for gi, w_ref in enumerate((wb_in, wc_in, wh_in)):
            wb_ref[:, gi * cw:(gi + 1) * cw] = w_ref[...].astype(BF16)

    w = cw_ref[...]
    gd = LANES
    def project(r):
        return jnp.dot(a_ref[r * rt:(r + 1) * rt, :], wb_ref[...], preferred_element_type=F32)

    tail = jnp.zeros((SUBLANES, cw), F32)
    n_groups = length // rt
    nxt = project(0)
    for r in range(n_groups):
        rows = slice(r * rt, (r + 1) * rt)
        pr = nxt
        if r + 1 < n_groups:
            nxt = project(r + 1)
        u = pr[:, cw:2 * cw] * pr[:, 2 * cw:]
        full = jnp.concatenate([tail, u], axis=0)
        u1 = pltpu.roll(full, 1, axis=0)[SUBLANES:]
        u2 = pltpu.roll(full, 2, axis=0)[SUBLANES:]
        tail = u[rt - SUBLANES:]
        y = pr[:, :cw] * (u2 * w[0:1] + u1 * w[1:2] + u * w[2:3])
        for gi in range(cw // gd):
            sl = slice(gi * gd, (gi + 1) * gd)
            o_ref[rows, sl] = _rms(y[:, sl], gn_ref[:, sl]).astype(o_ref.dtype)


def _conv_mixer(h, w_in, conv_w, gn, w_side1, w_side2, layer, bsz, length):
    t, d = h.shape
    dc = conv_w.shape[1]
    assert dc // CONV_GROUPS == LANES
    cw = _pick(dc, CONV_COLS)
    rt = _pick(length, CONV_ROWS)
    nj = dc // cw
    s1_in, s1_out, s1_shape = _side_cast_specs(w_side1, layer, nj * bsz, bsz)
    s2_in, s2_out, s2_shape = _side_cast_specs(w_side2, layer, nj * bsz, bsz)
    return pl.pallas_call(
        functools.partial(_conv_kernel, rt),
        grid=(nj, bsz),
        in_specs=[pl.BlockSpec((length, d), lambda j, b: (b, 0)),
                  _wtile(d, cw, layer, lambda j: j),
                  _wtile(d, cw, layer, lambda j: nj + j),
                  _wtile(d, cw, layer, lambda j: 2 * nj + j),
                  pl.BlockSpec((conv_w.shape[0], cw), lambda j, b: (0, j)),
                  pl.BlockSpec((1, cw), lambda j, b: (0, j)),
                  s1_in, s2_in],
        out_specs=[pl.BlockSpec((length, cw), lambda j, b: (b, j)), s1_out, s2_out],
        out_shape=[jax.ShapeDtypeStruct((t, dc), BF16), s1_shape, s2_shape],
        scratch_shapes=[pltpu.VMEM((d, 3 * cw), BF16)],
        compiler_params=_params("arbitrary", "arbitrary"),
        name="conv_mixer",
    )(h, w_in, w_in, w_in, conv_w, gn.reshape(1, dc), w_side1, w_side2)


def _level_table(c):
    t = jnp.arange(c, dtype=jnp.int32)[:, None]
    s = jnp.arange(c, dtype=jnp.int32)[None, :]
    x = t ^ s
    lvl = jnp.zeros((c, c), jnp.int32)
    m = 1
    while m < c:
        lvl = jnp.where(x >= m, m, lvl)
        m *= 2
    return jnp.where(t > s, lvl, 0)


def _tiles(x):
    return [x[j * SUBLANES:(j + 1) * SUBLANES, :] for j in range(x.shape[0] // SUBLANES)]


def _bcast_row(x, r):
    return jnp.broadcast_to(x[r:r + 1, :], x.shape)


def _hgrn_gates(q, z, v, g, one_m_lb):
    c, dk = q.shape
    nv = c // SUBLANES
    sub = lax.broadcasted_iota(jnp.int32, (SUBLANES, dk), 0)

    qf = _silu(q)
    k = one_m_lb * (1.0 / (1.0 + jnp.exp2(z * LOG2E)))
    fc = jnp.maximum(1.0 - k, MIN_FORGET)
    lf = jnp.log2(fc)

    qf_t, k_t, fc_t = _tiles(qf), _tiles(k), _tiles(fc)

    loc_t = []
    for x in _tiles(lf):
        for s in (1, 2, 4):
            x = x + jnp.where(sub >= s, pltpu.roll(x, s, axis=0), 0.0)
        loc_t.append(x)
    b_t = [loc_t[0]]
    for j in range(1, nv):
        b_t.append(loc_t[j] + _bcast_row(b_t[j - 1], SUBLANES - 1))
    b = jnp.concatenate(b_t, axis=0)
    b_last = b_t[nv - 1][SUBLANES - 1:SUBLANES, :]

    def level_operand(m):
        p_t = []
        if m == 1:
            odd = (sub & 1) == 1
            for j in range(nv):
                p_t.append(jnp.where(odd, qf_t[j] * fc_t[j], k_t[j]))
        elif m < SUBLANES:
            is_q = (sub & m) != 0
            for j in range(nv):
                x = loc_t[j]
                if m == 2:
                    ref = jnp.where(sub < 4, _bcast_row(x, 1), _bcast_row(x, 5))
                else:
                    ref = _bcast_row(x, 3)
                em = jnp.exp2(-jnp.abs(x - ref))
                p_t.append(jnp.where(is_q, qf_t[j], k_t[j]) * em)
        else:
            mv = m // SUBLANES
            for j0 in range(0, nv, 2 * mv):
                ref = _bcast_row(b_t[j0 + mv - 1], SUBLANES - 1)
                for j in range(j0, j0 + mv):
                    p_t.append(k_t[j] * jnp.exp2(ref - b_t[j]))
                for j in range(j0 + mv, j0 + 2 * mv):
                    p_t.append(qf_t[j] * jnp.exp2(b_t[j] - ref))
        return p_t

    levels = [1 << i for i in range(c.bit_length() - 1)]
    small = [m for m in levels if m < SUBLANES]
    big = [m for m in levels if m >= SUBLANES]
    groups = ([small[i:i + 2] for i in range(0, len(small) - len(small) % 2, 2)]
              + ([small[-1:]] if len(small) % 2 else [])
              + [big[i:i + 2] for i in range(0, len(big), 2)])
    operands = []
    for ms in groups:
        p_ts = [level_operand(m) for m in ms]
        ps = [jnp.concatenate(p_t, axis=0).astype(BF16) for p_t in p_ts]
        if ms[0] >= SUBLANES:
            q_tiles = [[j for j in range(nv) if j & (m // SUBLANES)] for m in ms]
            ls = [jnp.concatenate([p_ts[i][j] for j in q_tiles[i]], axis=0).astype(BF16)
                  for i in range(len(ms))]
        else:
            q_tiles = [list(range(nv)) for _ in ms]
            ls = ps
        operands.append((ms, ps, ls, q_tiles))
    diag = jnp.sum(qf * k, axis=-1, keepdims=True)
    qe = (qf * jnp.exp2(b)).astype(BF16)
    khat_t = (k * jnp.exp2(b_last - b)).astype(BF16).T
    decay = jnp.transpose(jnp.broadcast_to(jnp.exp2(b_last), (dk, dk)))
    return operands, qe, khat_t, v.astype(BF16), diag * v, decay, _silu(g)


def _hgrn_scores(gates, lvl):
    operands, qe, khat_t, v_bf, diag_v, decay, gate = gates
    c, dk = qe.shape
    zero = jnp.zeros((c, dk), BF16)
    lvl_t = _tiles(lvl)
    sc_t = [jnp.zeros((SUBLANES, c), F32) for _ in range(c // SUBLANES)]
    for ms, ps, ls, q_tiles in operands:
        if len(ps) == 2:
            lhs = jnp.concatenate(ls, axis=1)
            rhs = jnp.concatenate([jnp.concatenate([ps[0], zero], axis=1),
                                   jnp.concatenate([zero, ps[1]], axis=1)], axis=0)
            sp = _dot_t(lhs, rhs)
        else:
            sp = _dot_t(ls[0], ps[0])
        for i, m in enumerate(ms):
            for idx, j in enumerate(q_tiles[i]):
                rows = slice(idx * SUBLANES, (idx + 1) * SUBLANES)
                sc_t[j] = jnp.where(lvl_t[j] == m, sp[rows, i * c:(i + 1) * c], sc_t[j])
    scores = jnp.concatenate(sc_t, axis=0)
    lhs = jnp.concatenate([jnp.concatenate([qe, scores.astype(BF16)], axis=1),
                           jnp.concatenate([jnp.zeros((dk, dk), BF16), khat_t], axis=1)], axis=0)
    return lhs, v_bf, diag_v, decay, gate


def _hgrn_back(front, gn, st):
    lhs, v_bf, diag_v, decay, gate = front
    c = v_bf.shape[0]
    rhs = jnp.concatenate([st.astype(BF16), v_bf], axis=0)
    res = jnp.dot(lhs, rhs, preferred_element_type=F32)
    out = _rms(res[:c] + diag_v, gn) * gate
    return out, st * decay + res[c:]


def _hgrn_kernel(layer, c, rg, nh, a_ref, wq_in, wz_in, wv_in, wg_in, lbl_ref, gn_ref, lvl_ref,
                 o_ref, wb_ref):
    length, wd = o_ref.shape
    dk = wd // nh

    @pl.when(pl.program_id(1) == 0)
    def _():
        for gi, w_ref in enumerate((wq_in, wz_in, wv_in, wg_in)):
            wb_ref[:, gi * wd:(gi + 1) * wd] = w_ref[...].astype(BF16)

    lg = lbl_ref[...]
    ex = jnp.exp(lg - jnp.max(lg, axis=0, keepdims=True))
    p = ex / jnp.sum(ex, axis=0, keepdims=True)
    cum = jnp.sum(p[:layer + 1], axis=0, keepdims=True)
    lb = jnp.clip(cum - p[0:1], 0.0, 1.0 - 1e-4)
    one_m_lb = 1.0 - lb
    gn = gn_ref[...]
    lvl = lvl_ref[...]

    if length >= 4 * rg:
        sizes = [rg // 2] + [rg] * (length // rg - 1) + [rg // 2]
    else:
        sizes = [rg] * (length // rg)
    starts = [sum(sizes[:r]) for r in range(len(sizes))]

    pw = 2 * LANES
    n_pieces = 4 * wd // pw

    def project(r, j):
        return jnp.dot(a_ref[starts[r]:starts[r] + sizes[r], :], wb_ref[:, j * pw:(j + 1) * pw],
                       preferred_element_type=F32)

    def columns(pr, gi, e, rows):
        lo = gi * wd + e * dk
        return pr[lo // pw][rows, lo % pw:lo % pw + dk]

    def finish(ci, e, front):
        ls = slice(e * dk, (e + 1) * dk)
        out, sts[e] = _hgrn_back(front, gn[:, ls], sts[e])
        o_ref[ci * c:(ci + 1) * c, ls] = out.astype(o_ref.dtype)

    sts = [jnp.zeros((dk, dk), F32) for _ in range(nh)]
    n_groups = len(sizes)
    nxt = [project(0, j) for j in range(n_pieces)]
    gated, scored = [], []
    for r in range(n_groups):
        pr = nxt
        nxt = []
        for i in range(sizes[r] // c):
            rows = slice(i * c, (i + 1) * c)
            for e in range(nh):
                if r + 1 < n_groups and len(nxt) < n_pieces:
                    nxt.append(project(r + 1, len(nxt)))
                ls = slice(e * dk, (e + 1) * dk)
                gates = _hgrn_gates(*(columns(pr, gi, e, rows) for gi in range(4)), one_m_lb[:, ls])
                gated.append((starts[r] // c + i, e, gates))
                if len(gated) > HGRN_SKEW * nh:
                    ci, ce, gt = gated.pop(0)
                    scored.append((ci, ce, _hgrn_scores(gt, lvl)))
                if len(scored) > nh:
                    finish(*scored.pop(0))
        while r + 1 < n_groups and len(nxt) < n_pieces:
            nxt.append(project(r + 1, len(nxt)))
    for ci, ce, gt in gated:
        scored.append((ci, ce, _hgrn_scores(gt, lvl)))
        if len(scored) > nh:
            finish(*scored.pop(0))
    for item in scored:
        finish(*item)


def _hgrn_mixer(h, w_in, lb_logits, gn, layer, bsz, length, col0):
    t, d = h.shape
    depth, dh = lb_logits.shape
    dk = dh // HGRN_HEADS
    nh = HGRN_HEADS_PER_STEP
    wd = nh * dk
    c = min(HGRN_CHUNK, length)
    assert HGRN_HEADS % nh == 0 and col0 % nh == 0

    def wcol(gi):
        return _wtile(d, wd, layer, lambda hp: (col0 + gi * HGRN_HEADS) // nh + hp)

    return pl.pallas_call(
        functools.partial(_hgrn_kernel, layer, c, min(HGRN_ROW_GROUP, length), nh),
        grid=(HGRN_HEADS // nh, bsz),
        in_specs=[pl.BlockSpec((length, d), lambda hp, b: (b, 0)),
                  wcol(0), wcol(1), wcol(2), wcol(3),
                  pl.BlockSpec((depth, wd), lambda hp, b: (0, hp)),
                  pl.BlockSpec((1, wd), lambda hp, b: (0, hp)),
                  pl.BlockSpec((c, c), lambda hp, b: (0, 0))],
        out_specs=pl.BlockSpec((length, wd), lambda hp, b: (b, hp)),
        out_shape=jax.ShapeDtypeStruct((t, dh), BF16),
        scratch_shapes=[pltpu.VMEM((d, 4 * wd), BF16)],
        compiler_params=_params("arbitrary", "arbitrary"),
        name="hgrn_mixer",
    )(h, w_in, w_in, w_in, w_in, lb_logits, gn.reshape(1, dh), _level_table(c))


def kernel(x, norm_mix, w_in, conv_w, gn_conv, lb_logits, gn_hgrn, w_out, norm_ffn,
           w_gate, w_up, w_down, norm_final):
    bsz, length, d = x.shape
    depth = w_in.shape[0]
    dc = conv_w.shape[-1]
    dh = lb_logits.shape[-1]
    t = bsz * length
    hgrn_col0 = (3 * dc) // (dh // HGRN_HEADS)

    xr = x.reshape(t, d)
    h = _norm(xr, norm_mix[0], BF16)
    for l in range(depth):
        conv_out, w_out_bf, w_down_bf = _conv_mixer(h, w_in, conv_w[l], gn_conv[l], w_out, w_down,
                                                    l, bsz, length)
        hgrn_out = _hgrn_mixer(h, w_in, lb_logits, gn_hgrn[l], l, bsz, length, hgrn_col0)
        xr, h = _out_proj(conv_out, hgrn_out, w_out_bf, xr, norm_ffn[l])
        act = _glu(h, w_gate, w_up, l)
        last = l == depth - 1
        if last:
            y = _down_proj(act, w_down_bf, xr, norm_final, True)
        else:
            xr, h = _down_proj(act, w_down_bf, xr, norm_mix[l + 1], False)
    return y.reshape(bsz, length, d)
```
